```python
import functools
import jax
import jax.numpy as jnp
from jax import lax
import numpy as np

D_MODEL = 1024
BATCH = 2
SEQ = 8192
DEPTH = 1
DEC_BATCH = 128
DEC_SEQ = 1
PAST_LEN = 8192
PAGE_SIZE = 128

D_RNN = 1280
RNN_BLOCKS = 16
RNN_BLOCK = D_RNN // RNN_BLOCKS
CONV_W = 4
LRU_C = 8.0
N_HEADS = 16
QK_NOPE = 64
QK_ROPE = 32
V_HEAD = 64
Q_LORA = 384
KV_LORA = 256
ROPE_THETA = 10000.0
SM_SCALE = (QK_NOPE + QK_ROPE) ** -0.5
Q_BLOCK = 128
D_FF = 2816
D_PLE = 256
EPS = 1e-6
O_XR = D_RNN
O_YR = O_XR + D_RNN
O_Q = O_YR + Q_LORA
O_KV = O_Q + KV_LORA
O_KR = O_KV + QK_ROPE
O_GA = O_KR + D_MODEL
D_IN = O_GA + D_MODEL

kernel_name = 'hybrid_rglru_mla_macaron_step'


def rmsnorm(x, g):
    xf = x.astype(jnp.float32)
    y = xf * lax.rsqrt(jnp.mean(xf * xf, axis=-1, keepdims=True) + EPS)
    return (y * g.astype(jnp.float32)).astype(x.dtype)


def swiglu(u, w_gate, w_up, w_down):
    return (jax.nn.silu(u @ w_gate) * (u @ w_up)) @ w_down


def rope(x, pos):
    half = QK_ROPE // 2
    freqs = ROPE_THETA ** (-jnp.arange(half, dtype=jnp.float32) / half)
    ang = pos.astype(jnp.float32)[:, None] * freqs[None, :]
    cos = jnp.cos(ang)[:, None, :]
    sin = jnp.sin(ang)[:, None, :]
    xf = x.astype(jnp.float32)
    x1, x2 = xf[..., :half], xf[..., half:]
    return jnp.concatenate([x1 * cos - x2 * sin, x1 * sin + x2 * cos], axis=-1).astype(x.dtype)


def causal_conv(x, buf, w, b):
    s = x.shape[1]
    xe = jnp.concatenate([buf.astype(x.dtype), x], axis=1)
    y = b + xe[:, 0:s] * w[0]
    for k in range(1, CONV_W):
        y = y + xe[:, k:k + s] * w[k]
    return y, xe[:, -(CONV_W - 1):]


def rg_lru(x, h0, w_a, b_a, w_i, b_i, lam):
    bsz, s, _ = x.shape
    xb = x.reshape(bsz, s, RNN_BLOCKS, RNN_BLOCK)
    r = jax.nn.sigmoid(jnp.einsum('bsnc,ncd->bsnd', xb, w_a) + b_a).reshape(bsz, s, D_RNN)
    gi = jax.nn.sigmoid(jnp.einsum('bsnc,ncd->bsnd', xb, w_i) + b_i).reshape(bsz, s, D_RNN)
    log_a = -LRU_C * r.astype(jnp.float32) * jax.nn.softplus(-lam.astype(jnp.float32))
    a = jnp.exp(log_a)
    mult = jnp.sqrt(-jnp.expm1(2.0 * log_a))
    bt = mult * (gi * x).astype(jnp.float32)
    bt = bt.at[:, 0].add(a[:, 0] * h0.astype(jnp.float32))

    def combine(left, right):
        a1, b1 = left
        a2, b2 = right
        return a1 * a2, a2 * b1 + b2

    _, h = lax.associative_scan(combine, (a, bt), axis=1)
    return h.astype(x.dtype), h[:, -1].astype(h0.dtype)


def mla_project(cq_in, kv_in, kr_in, pos, q_norm, w_uq, w_qr, kv_norm, w_uk):
    cq = rmsnorm(cq_in, q_norm)
    q_nope = jnp.einsum('bsr,rhd->bshd', cq, w_uq)
    q_rope = rope(jnp.einsum('bsr,rhd->bshd', cq, w_qr), pos)
    ckv = rmsnorm(kv_in, kv_norm)
    k_rope = rope(kr_in[:, :, None, :], pos)[:, :, 0]
    q_abs = jnp.einsum('bshd,rhd->bshr', q_nope, w_uk)
    return q_abs, q_rope, ckv, k_rope


def mla_scores(q_abs, q_rope, ckv, krope):
    s = jnp.einsum('bqhr,bkr->bhqk', q_abs, ckv) + jnp.einsum('bqhd,bkd->bhqk', q_rope, krope)
    return s.astype(jnp.float32) * SM_SCALE


def mla_attend_prompt(q_abs, q_rope, ckv, krope):
    bsz, s = q_abs.shape[0], q_abs.shape[1]
    nb = s // Q_BLOCK
    qa = q_abs.reshape(bsz, nb, Q_BLOCK, N_HEADS, KV_LORA).transpose(1, 0, 2, 3, 4)
    qr = q_rope.reshape(bsz, nb, Q_BLOCK, N_HEADS, QK_ROPE).transpose(1, 0, 2, 3, 4)
    kpos = jnp.arange(s)

    def block(args):
        qa_b, qr_b, start = args
        sc = mla_scores(qa_b, qr_b, ckv, krope)
        qpos = start + jnp.arange(Q_BLOCK)
        sc = jnp.where(kpos[None, :] <= qpos[:, None], sc, -jnp.inf)
        p = jax.nn.softmax(sc, axis=-1).astype(ckv.dtype)
        return jnp.einsum('bhqk,bkr->bqhr', p, ckv)

    o = lax.map(block, (qa, qr, jnp.arange(nb) * Q_BLOCK))
    return o.transpose(1, 0, 2, 3, 4).reshape(bsz, s, N_HEADS, KV_LORA)


def mla_attend_sample(q_abs, q_rope, ckv_new, kr_new, ckv_pool, kr_pool, page_table):
    bd, q = q_abs.shape[0], q_abs.shape[1]
    ckv_past = ckv_pool[page_table].reshape(bd, -1, KV_LORA).astype(ckv_new.dtype)
    kr_past = kr_pool[page_table].reshape(bd, -1, QK_ROPE).astype(kr_new.dtype)
    s_past = mla_scores(q_abs, q_rope, ckv_past, kr_past)
    s_new = mla_scores(q_abs, q_rope, ckv_new, kr_new)
    s_new = jnp.where(jnp.tril(jnp.ones((q, q), dtype=bool)), s_new, -jnp.inf)
    n_past = s_past.shape[-1]
    p = jax.nn.softmax(jnp.concatenate([s_past, s_new], axis=-1), axis=-1).astype(ckv_new.dtype)
    return (jnp.einsum('bhqk,bkr->bqhr', p[..., :n_past], ckv_past)
            + jnp.einsum('bhqk,bkr->bqhr', p[..., n_past:], ckv_new))


def decoder_layer(x, p, pos, h0, conv_buf, attend, lw):
    u = rmsnorm(x, lw['ffn1_pre'])
    x = x + 0.5 * rmsnorm(swiglu(u, lw['ffn1_w_gate'], lw['ffn1_w_up'], lw['ffn1_w_down']), lw['ffn1_post'])
    u = rmsnorm(x, lw['mix_pre'])
    z = u @ lw['w_in']
    xr, yr, cq_in, kv_in, kr_in = z[..., :O_XR], z[..., O_XR:O_YR], z[..., O_YR:O_Q], z[..., O_Q:O_KV], z[..., O_KV:O_KR]
    g_a, g_b = z[..., O_KR:O_GA], z[..., O_GA:D_IN]
    xc, conv_new = causal_conv(xr, conv_buf, lw['conv_w'], lw['conv_b'])
    hseq, h_last = rg_lru(xc, h0, lw['lru_w_a'], lw['lru_b_a'], lw['lru_w_i'], lw['lru_b_i'], lw['lru_lambda'])
    y_a = (hseq * jax.nn.gelu(yr)) @ lw['w_branch_rnn']
    q_abs, q_rope, ckv, krope = mla_project(cq_in, kv_in, kr_in, pos, lw['q_norm'], lw['w_uq'], lw['w_qr'], lw['kv_norm'], lw['w_uk'])
    o_lat = attend(q_abs, q_rope, ckv, krope)
    o = jnp.einsum('bshr,rhv->bshv', o_lat, lw['w_uv'])
    y_b = o.reshape(o.shape[0], o.shape[1], N_HEADS * V_HEAD) @ lw['w_branch_attn']
    m = jax.nn.sigmoid(g_a) * y_a + jax.nn.sigmoid(g_b) * y_b
    x = x + rmsnorm(m @ lw['w_out'], lw['mix_post'])
    u = rmsnorm(x, lw['ffn2_pre'])
    x = x + 0.5 * rmsnorm(swiglu(u, lw['ffn2_w_gate'], lw['ffn2_w_up'], lw['ffn2_w_down']), lw['ffn2_post'])
    e = jax.nn.sigmoid(x @ lw['ple_gate']) * (p @ lw['ple_proj'])
    x = x + rmsnorm(e, lw['ple_post'])
    return x, (ckv, krope, h_last, conv_new)


def setup_inputs(seed: int = 0) -> dict:
    key = jax.random.key(seed)
    ks = list(jax.random.split(key, 64))
    f32 = jnp.float32

    def nrm(shape, scale):
        return jax.random.normal(ks.pop(), shape, f32) * scale

    def gain(n):
        return 1.0 + nrm((DEPTH, n), 0.05)

    n_pages = PAST_LEN // PAGE_SIZE
    n_used = DEC_BATCH * n_pages
    n_pool = (n_used * 5) // 4
    page_table = jax.random.permutation(ks.pop(), n_pool)[:n_used].reshape(DEC_BATCH, n_pages).astype(jnp.int32)
    a0 = jax.random.uniform(ks.pop(), (DEPTH, D_RNN), f32, 0.9, 0.999)
    lru_lambda = jnp.log(a0) - jnp.log1p(-a0)
    return {
        'x_prompt': nrm((BATCH, SEQ, D_MODEL), 1.0),
        'x_sample': nrm((DEC_BATCH, DEC_SEQ, D_MODEL), 1.0),
        'p_prompt': nrm((DEPTH, BATCH, SEQ, D_PLE), 1.0),
        'p_sample': nrm((DEPTH, DEC_BATCH, DEC_SEQ, D_PLE), 1.0),
        'cache_ckv': nrm((DEPTH, n_pool, PAGE_SIZE, KV_LORA), 1.0),
        'cache_krope': nrm((DEPTH, n_pool, PAGE_SIZE, QK_ROPE), 1.0),
        'state_h': nrm((DEPTH, DEC_BATCH, D_RNN), 0.5),
        'state_conv': nrm((DEPTH, DEC_BATCH, CONV_W - 1, D_RNN), 1.0),
        'page_table': page_table,
        'ffn1_pre': gain(D_MODEL),
        'ffn1_w_gate': nrm((DEPTH, D_MODEL, D_FF), D_MODEL ** -0.5),
        'ffn1_w_up': nrm((DEPTH, D_MODEL, D_FF), D_MODEL ** -0.5),
        'ffn1_w_down': nrm((DEPTH, D_FF, D_MODEL), D_FF ** -0.5),
        'ffn1_post': gain(D_MODEL),
        'mix_pre': gain(D_MODEL),
        'w_in': nrm((DEPTH, D_MODEL, D_IN), D_MODEL ** -0.5),
        'conv_w': nrm((DEPTH, CONV_W, D_RNN), CONV_W ** -0.5),
        'conv_b': nrm((DEPTH, D_RNN), 0.02),
        'lru_w_a': nrm((DEPTH, RNN_BLOCKS, RNN_BLOCK, RNN_BLOCK), RNN_BLOCK ** -0.5),
        'lru_b_a': nrm((DEPTH, RNN_BLOCKS, RNN_BLOCK), 0.02),
        'lru_w_i': nrm((DEPTH, RNN_BLOCKS, RNN_BLOCK, RNN_BLOCK), RNN_BLOCK ** -0.5),
        'lru_b_i': nrm((DEPTH, RNN_BLOCKS, RNN_BLOCK), 0.02),
        'lru_lambda': lru_lambda,
        'w_branch_rnn': nrm((DEPTH, D_RNN, D_MODEL), D_RNN ** -0.5),
        'q_norm': gain(Q_LORA),
        'w_uq': nrm((DEPTH, Q_LORA, N_HEADS, QK_NOPE), Q_LORA ** -0.5),
        'w_qr': nrm((DEPTH, Q_LORA, N_HEADS, QK_ROPE), Q_LORA ** -0.5),
        'kv_norm': gain(KV_LORA),
        'w_uk': nrm((DEPTH, KV_LORA, N_HEADS, QK_NOPE), KV_LORA ** -0.5),
        'w_uv': nrm((DEPTH, KV_LORA, N_HEADS, V_HEAD), KV_LORA ** -0.5),
        'w_branch_attn': nrm((DEPTH, N_HEADS * V_HEAD, D_MODEL), (N_HEADS * V_HEAD) ** -0.5),
        'w_out': nrm((DEPTH, D_MODEL, D_MODEL), D_MODEL ** -0.5),
        'mix_post': gain(D_MODEL),
        'ffn2_pre': gain(D_MODEL),
        'ffn2_w_gate': nrm((DEPTH, D_MODEL, D_FF), D_MODEL ** -0.5),
        'ffn2_w_up': nrm((DEPTH, D_MODEL, D_FF), D_MODEL ** -0.5),
        'ffn2_w_down': nrm((DEPTH, D_FF, D_MODEL), D_FF ** -0.5),
        'ffn2_post': gain(D_MODEL),
        'ple_gate': nrm((DEPTH, D_MODEL, D_MODEL), D_MODEL ** -0.5),
        'ple_proj': nrm((DEPTH, D_PLE, D_MODEL), D_PLE ** -0.5),
        'ple_post': gain(D_MODEL),
    }


def reference(x_prompt, x_sample, p_prompt, p_sample, cache_ckv, cache_krope, state_h, state_conv, page_table,
              ffn1_pre, ffn1_w_gate, ffn1_w_up, ffn1_w_down, ffn1_post,
              mix_pre, w_in, conv_w, conv_b, lru_w_a, lru_b_a, lru_w_i, lru_b_i, lru_lambda, w_branch_rnn,
              q_norm, w_uq, w_qr, kv_norm, w_uk, w_uv, w_branch_attn, w_out, mix_post,
              ffn2_pre, ffn2_w_gate, ffn2_w_up, ffn2_w_down, ffn2_post,
              ple_gate, ple_proj, ple_post):
    bsz = x_prompt.shape[0]
    pos_prompt = jnp.arange(x_prompt.shape[1], dtype=jnp.int32)
    past_len = page_table.shape[1] * PAGE_SIZE
    pos_sample = past_len + jnp.arange(x_sample.shape[1], dtype=jnp.int32)
    h0_prompt = jnp.zeros((bsz, D_RNN), state_h.dtype)
    conv0_prompt = jnp.zeros((bsz, CONV_W - 1, D_RNN), x_prompt.dtype)
    hp, hs = x_prompt, x_sample
    st_p_all, st_s_all = [], []
    for i in range(DEPTH):
        lw = dict(
            ffn1_pre=ffn1_pre[i], ffn1_w_gate=ffn1_w_gate[i], ffn1_w_up=ffn1_w_up[i], ffn1_w_down=ffn1_w_down[i], ffn1_post=ffn1_post[i],
            mix_pre=mix_pre[i], w_in=w_in[i], conv_w=conv_w[i], conv_b=conv_b[i],
            lru_w_a=lru_w_a[i], lru_b_a=lru_b_a[i], lru_w_i=lru_w_i[i], lru_b_i=lru_b_i[i], lru_lambda=lru_lambda[i],
            w_branch_rnn=w_branch_rnn[i], q_norm=q_norm[i], w_uq=w_uq[i], w_qr=w_qr[i], kv_norm=kv_norm[i],
            w_uk=w_uk[i], w_uv=w_uv[i], w_branch_attn=w_branch_attn[i], w_out=w_out[i], mix_post=mix_post[i],
            ffn2_pre=ffn2_pre[i], ffn2_w_gate=ffn2_w_gate[i], ffn2_w_up=ffn2_w_up[i], ffn2_w_down=ffn2_w_down[i], ffn2_post=ffn2_post[i],
            ple_gate=ple_gate[i], ple_proj=ple_proj[i], ple_post=ple_post[i])
        hp, st_p = decoder_layer(hp, p_prompt[i], pos_prompt, h0_prompt, conv0_prompt, mla_attend_prompt, lw)
        attend_s = functools.partial(mla_attend_sample, ckv_pool=cache_ckv[i], kr_pool=cache_krope[i], page_table=page_table)
        hs, st_s = decoder_layer(hs, p_sample[i], pos_sample, state_h[i], state_conv[i], attend_s, lw)
        st_p_all.append(st_p)
        st_s_all.append(st_s)
    new_ckv_prompt = jnp.stack([s[0] for s in st_p_all])
    new_krope_prompt = jnp.stack([s[1] for s in st_p_all])
    new_h_prompt = jnp.stack([s[2] for s in st_p_all])
    new_conv_prompt = jnp.stack([s[3] for s in st_p_all])
    new_ckv_sample = jnp.stack([s[0] for s in st_s_all])
    new_krope_sample = jnp.stack([s[1] for s in st_s_all])
    new_h_sample = jnp.stack([s[2] for s in st_s_all])
    new_conv_sample = jnp.stack([s[3] for s in st_s_all])
    return (hp, hs, new_ckv_prompt, new_krope_prompt, new_h_prompt, new_conv_prompt,
            new_ckv_sample, new_krope_sample, new_h_sample, new_conv_sample)
```

```python
import functools

import jax
import jax.numpy as jnp
from jax import lax
from jax.experimental import pallas as pl
from jax.experimental.pallas import tpu as pltpu

D_MODEL = 1024
D_RNN = 1280
RNN_BLOCKS = 16
RNN_BLOCK = D_RNN // RNN_BLOCKS
CONV_W = 4
LRU_C = 8.0
N_HEADS = 16
QK_NOPE = 64
QK_ROPE = 32
V_HEAD = 64
Q_LORA = 384
KV_LORA = 256
ROPE_THETA = 10000.0
SM_SCALE = (QK_NOPE + QK_ROPE) ** -0.5
D_FF = 2816
D_PLE = 256
EPS = 1e-6
PAGE_SIZE = 128
O_XR = D_RNN
O_YR = O_XR + D_RNN
O_Q = O_YR + Q_LORA
O_KV = O_Q + KV_LORA
O_KR = O_KV + QK_ROPE

LANES = 128
HEAD_PAD = LANES
VMEM_LIMIT_BYTES = 56 * 1024 * 1024

F32 = jnp.float32
BF16 = jnp.bfloat16


def _rms(x, g):
    return x * lax.rsqrt(jnp.mean(x * x, axis=-1, keepdims=True) + EPS) * g


def _dot(a, b):
    return jnp.dot(a, b, preferred_element_type=F32)


def _dot_nt(a, b):
    return lax.dot_general(a, b, (((1,), (1,)), ((), ())), preferred_element_type=F32)


def _resident(shape):
    nd = len(shape)
    return pl.BlockSpec(shape, lambda *_: (0,) * nd, pipeline_mode=pl.Buffered(1))


def _params(*sem):
    return pltpu.CompilerParams(dimension_semantics=sem, vmem_limit_bytes=VMEM_LIMIT_BYTES)


def _ffn_kernel(x_ref, pre_ref, wg_ref, wu_ref, wd_ref, post_ref, o_ref, *, f_chunk):
    x = x_ref[...]
    u = _rms(x, pre_ref[...]).astype(BF16)
    acc = jnp.zeros(x.shape, F32)
    for c in range(D_FF // f_chunk):
        sl = slice(c * f_chunk, (c + 1) * f_chunk)
        g = _dot(u, wg_ref[:, sl])
        h = (g * jax.nn.sigmoid(g)) * _dot(u, wu_ref[:, sl])
        acc = acc + _dot(h.astype(BF16), wd_ref[sl, :])
    o_ref[...] = x + 0.5 * _rms(acc, post_ref[...])


def _ffn(x, pre, wg, wu, wd, post, *, tm):
    n = x.shape[0]
    return pl.pallas_call(
        functools.partial(_ffn_kernel, f_chunk=D_FF // 2),
        grid=(n // tm,),
        in_specs=[pl.BlockSpec((tm, D_MODEL), lambda i: (i, 0)),
                  _resident((1, D_MODEL)), _resident((D_MODEL, D_FF)), _resident((D_MODEL, D_FF)),
                  _resident((D_FF, D_MODEL)), _resident((1, D_MODEL))],
        out_specs=pl.BlockSpec((tm, D_MODEL), lambda i: (i, 0)),
        out_shape=jax.ShapeDtypeStruct((n, D_MODEL), F32),
        compiler_params=_params("parallel"),
        name="ffn",
    )(x, pre, wg, wu, wd, post)


N_MAIN = 2 * D_RNN + Q_LORA + KV_LORA + 2 * D_MODEL
M_YR = D_RNN
M_CQ = 2 * D_RNN
M_KV = M_CQ + Q_LORA
M_GA = M_KV + KV_LORA
M_GB = M_GA + D_MODEL


def _proj_kernel(*refs, prompt):
    (x_ref, pre_ref, wmain_ref, qn_ref, kvn_ref, wkr_ref, cos32_ref, sin32_ref), refs = refs[:8], refs[8:]
    if prompt:
        (wq_ref, wqrot_ref, cos128_ref, sin128_ref, wukt_ref, wkrt_ref, cost_ref, sint_ref, wuv_ref), refs = refs[:9], refs[9:]
    else:
        (wuq_ref, wqr_ref, wqrrot_ref, cos512_ref, sin512_ref), refs = refs[:5], refs[5:]
    xr_ref, gy_ref, ckv_ref, kr_ref, ga_ref, gb_ref = refs[:6]

    u = _rms(x_ref[0], pre_ref[...]).astype(BF16)
    xr_ref[0] = _dot(u, wmain_ref[:, 0:M_YR])
    gy_ref[0] = jax.nn.gelu(_dot(u, wmain_ref[:, M_YR:M_CQ])).astype(BF16)
    ga_ref[0] = jax.nn.sigmoid(_dot(u, wmain_ref[:, M_GA:M_GB])).astype(BF16)
    gb_ref[0] = jax.nn.sigmoid(_dot(u, wmain_ref[:, M_GB:N_MAIN])).astype(BF16)
    cq = _rms(_dot(u, wmain_ref[:, M_CQ:M_KV]), qn_ref[...]).astype(BF16)
    ckv = _rms(_dot(u, wmain_ref[:, M_KV:M_GA]), kvn_ref[...])
    ckv_ref[0] = ckv
    ckv_b = ckv.astype(BF16)
    kr2 = _dot(u, wkr_ref[...])
    kr_ref[0] = kr2[:, :QK_ROPE] * cos32_ref[...] + kr2[:, QK_ROPE:] * sin32_ref[...]

    if prompt:
        q_ref, kt_ref, v_ref = refs[6:]
        cos128 = cos128_ref[...]
        sin128 = sin128_ref[...]
        qa = _dot(cq, wq_ref[...])
        qb = _dot(cq, wqrot_ref[...])
        v = _dot(ckv_b, wuv_ref[...])
        kt_nope = _dot_nt(wukt_ref[...], ckv_b)
        krt2 = _dot_nt(wkrt_ref[...], u)
        krt = krt2[:HEAD_PAD] * cost_ref[...] + krt2[HEAD_PAD:] * sint_ref[...]
        for h in range(N_HEADS):
            sl = slice(h * HEAD_PAD, (h + 1) * HEAD_PAD)
            q_ref[0, h] = (qa[:, sl] * cos128 + qb[:, sl] * sin128).astype(BF16)
            v_ref[0, h] = v[:, sl].astype(BF16)
            kt_ref[0, sl, :] = (kt_nope[sl, :] + krt).astype(BF16)
    else:
        qn_out_ref, qr_out_ref = refs[6:]
        qn_out_ref[0] = (_dot(cq, wuq_ref[...]) * SM_SCALE).astype(BF16)
        qr_out_ref[0] = (_dot(cq, wqr_ref[...]) * cos512_ref[...]
                         + _dot(cq, wqrrot_ref[...]) * sin512_ref[...]).astype(BF16)


def _proj(x, w, tabs, *, prompt, tm):
    b, s, _ = x.shape
    row = lambda d: pl.BlockSpec((1, tm, d), lambda bi, i: (bi, i, 0))
    tab = lambda d: pl.BlockSpec((tm, d), lambda bi, i: (i, 0))
    tab_t = pl.BlockSpec((HEAD_PAD, tm), lambda bi, i: (0, i))
    in_specs = [row(D_MODEL), _resident((1, D_MODEL)), _resident((D_MODEL, N_MAIN)), _resident((1, Q_LORA)),
                _resident((1, KV_LORA)), _resident((D_MODEL, 2 * QK_ROPE)), tab(QK_ROPE), tab(QK_ROPE)]
    args = [x, w["mix_pre"], w["w_main"], w["q_norm"], w["kv_norm"], w["w_kr2"], tabs["cos32"], tabs["sin32"]]
    out_specs = [row(D_RNN), row(D_RNN), row(KV_LORA), row(QK_ROPE), row(D_MODEL), row(D_MODEL)]
    out_shape = [jax.ShapeDtypeStruct((b, s, D_RNN), F32), jax.ShapeDtypeStruct((b, s, D_RNN), BF16),
                 jax.ShapeDtypeStruct((b, s, KV_LORA), F32), jax.ShapeDtypeStruct((b, s, QK_ROPE), F32),
                 jax.ShapeDtypeStruct((b, s, D_MODEL), BF16), jax.ShapeDtypeStruct((b, s, D_MODEL), BF16)]
    hp = N_HEADS * HEAD_PAD
    if prompt:
        in_specs += [_resident((Q_LORA, hp)), _resident((Q_LORA, hp)), tab(HEAD_PAD), tab(HEAD_PAD),
                     _resident((hp, KV_LORA)), _resident((2 * HEAD_PAD, D_MODEL)), tab_t, tab_t,
                     _resident((KV_LORA, hp))]
        args += [w["w_q"], w["w_q_rot"], tabs["cos128"], tabs["sin128"], w["w_ukt"], w["w_krt2"],
                 tabs["cos_t"], tabs["sin_t"], w["w_uv_pad"]]
        head_major = pl.BlockSpec((1, N_HEADS, tm, HEAD_PAD), lambda bi, i: (bi, 0, i, 0))
        out_specs += [head_major, pl.BlockSpec((1, hp, tm), lambda bi, i: (bi, 0, i)), head_major]
        out_shape += [jax.ShapeDtypeStruct((b, N_HEADS, s, HEAD_PAD), BF16),
                      jax.ShapeDtypeStruct((b, hp, s), BF16),
                      jax.ShapeDtypeStruct((b, N_HEADS, s, HEAD_PAD), BF16)]
    else:
        nn, nr = N_HEADS * QK_NOPE, N_HEADS * QK_ROPE
        in_specs += [_resident((Q_LORA, nn)), _resident((Q_LORA, nr)), _resident((Q_LORA, nr)), tab(nr), tab(nr)]
        args += [w["w_uq"], w["w_qr"], w["w_qr_rot"], tabs["cos512"], tabs["sin512"]]
        out_specs += [row(nn), row(nr)]
        out_shape += [jax.ShapeDtypeStruct((b, s, nn), BF16), jax.ShapeDtypeStruct((b, s, nr), BF16)]
    return pl.pallas_call(
        functools.partial(_proj_kernel, prompt=prompt),
        grid=(b, s // tm),
        in_specs=in_specs, out_specs=out_specs, out_shape=out_shape,
        compiler_params=_params("parallel", "parallel"),
        name="proj_prompt" if prompt else "proj_sample",
    )(*args)


def _softplus(x):
    return jnp.maximum(x, 0.0) + jnp.log1p(jnp.exp(-jnp.abs(x)))


def _lru_coeffs(xc, wbd_ref, bab_ref, lam_ref):
    z = _dot(xc.astype(BF16), wbd_ref[...]) + bab_ref[...]
    r = jax.nn.sigmoid(z[:, :D_RNN])
    gi = jax.nn.sigmoid(z[:, D_RNN:])
    log_a = -LRU_C * r * _softplus(-lam_ref[...])
    a = jnp.exp(log_a)
    th = jnp.tanh(log_a)
    mult = jnp.sqrt(-2.0 * th / (1.0 - th))
    return a, mult * (gi * xc)


def _rglru_prompt_kernel(xr_ref, gy_ref, cw_ref, cb_ref, wbd_ref, bab_ref, lam_ref,
                         hg_ref, hlast_ref, convnew_ref, xbuf, a_s, b_s, h_s, hcar, *, ts):
    t = pl.program_id(1)
    halo = 8

    @pl.when(t == 0)
    def _():
        xbuf[0:halo, :] = jnp.zeros((halo, D_RNN), F32)
        hcar[...] = jnp.zeros((1, D_RNN), F32)

    x = xr_ref[0]
    xbuf[halo:halo + ts, :] = x
    xc = cb_ref[...] + xbuf[halo - 3:halo - 3 + ts, :] * cw_ref[0:1, :]
    xc = xc + xbuf[halo - 2:halo - 2 + ts, :] * cw_ref[1:2, :]
    xc = xc + xbuf[halo - 1:halo - 1 + ts, :] * cw_ref[2:3, :]
    xc = xc + x * cw_ref[3:4, :]
    xbuf[0:halo, :] = x[ts - halo:, :]

    a, b = _lru_coeffs(xc, wbd_ref, bab_ref, lam_ref)
    a_s[...] = a
    b_s[...] = b

    def step(i, h):
        h = a_s[pl.ds(i, 1), :] * h + b_s[pl.ds(i, 1), :]
        h_s[pl.ds(i, 1), :] = h
        return h

    h = lax.fori_loop(0, ts, step, hcar[...], unroll=8)
    hcar[...] = h
    hg_ref[0] = (h_s[...] * gy_ref[0].astype(F32)).astype(BF16)

    @pl.when(t == pl.num_programs(1) - 1)
    def _():
        hlast_ref[0] = h
        convnew_ref[0] = x[ts - (CONV_W - 1):, :]


def _rglru_prompt(xr, gy, w, *, ts):
    b, s, _ = xr.shape
    row = pl.BlockSpec((1, ts, D_RNN), lambda bi, t: (bi, t, 0))
    return pl.pallas_call(
        functools.partial(_rglru_prompt_kernel, ts=ts),
        grid=(b, s // ts),
        in_specs=[row, row, _resident((CONV_W, D_RNN)), _resident((1, D_RNN)), _resident((D_RNN, 2 * D_RNN)),
                  _resident((1, 2 * D_RNN)), _resident((1, D_RNN))],
        out_specs=[row, pl.BlockSpec((1, 1, D_RNN), lambda bi, t: (bi, 0, 0)),
                   pl.BlockSpec((1, CONV_W - 1, D_RNN), lambda bi, t: (bi, 0, 0))],
        out_shape=[jax.ShapeDtypeStruct((b, s, D_RNN), BF16), jax.ShapeDtypeStruct((b, 1, D_RNN), F32),
                   jax.ShapeDtypeStruct((b, CONV_W - 1, D_RNN), F32)],
        scratch_shapes=[pltpu.VMEM((ts + 8, D_RNN), F32), pltpu.VMEM((ts, D_RNN), F32),
                        pltpu.VMEM((ts, D_RNN), F32), pltpu.VMEM((ts, D_RNN), F32), pltpu.VMEM((1, D_RNN), F32)],
        compiler_params=_params("parallel", "arbitrary"),
        name="rglru_prompt",
    )(xr, gy, w["conv_w"], w["conv_b"], w["w_lru"], w["b_lru"], w["lam"])


def _rglru_sample_kernel(xr_ref, gy_ref, sc_ref, h0_ref, cw_ref, cb_ref, wbd_ref, bab_ref, lam_ref,
                         hg_ref, hnew_ref, convnew_ref):
    x = xr_ref[...]
    xc = cb_ref[...] + sc_ref[0] * cw_ref[0:1, :]
    xc = xc + sc_ref[1] * cw_ref[1:2, :]
    xc = xc + sc_ref[2] * cw_ref[2:3, :]
    xc = xc + x * cw_ref[3:4, :]
    a, b = _lru_coeffs(xc, wbd_ref, bab_ref, lam_ref)
    h = a * h0_ref[...] + b
    hnew_ref[...] = h
    hg_ref[...] = (h * gy_ref[...].astype(F32)).astype(BF16)
    convnew_ref[0] = sc_ref[1]
    convnew_ref[1] = sc_ref[2]
    convnew_ref[2] = x


def _rglru_sample(xr, gy, sc, h0, w):
    n = xr.shape[0]
    return pl.pallas_call(
        _rglru_sample_kernel,
        out_shape=[jax.ShapeDtypeStruct((n, D_RNN), BF16), jax.ShapeDtypeStruct((n, D_RNN), F32),
                   jax.ShapeDtypeStruct((CONV_W - 1, n, D_RNN), F32)],
        compiler_params=pltpu.CompilerParams(vmem_limit_bytes=VMEM_LIMIT_BYTES),
        name="rglru_sample",
    )(xr, gy, sc, h0, w["conv_w"], w["conv_b"], w["w_lru"], w["b_lru"], w["lam"])


def _attn_prompt_kernel(q_ref, kt_ref, v_ref, o_ref, m_s, l_s, acc_s, *, tq, tk):
    qi = pl.program_id(2)
    q = q_ref[0, 0]
    m_s[...] = jnp.full(m_s.shape, -jnp.inf, F32)
    l_s[...] = jnp.zeros(l_s.shape, F32)
    acc_s[...] = jnp.zeros(acc_s.shape, F32)

    def update(j, masked):
        k0 = pl.multiple_of(j * tk, tk)
        s = _dot(q, kt_ref[0, :, pl.ds(k0, tk)])
        if masked:
            qpos = qi * tq + lax.broadcasted_iota(jnp.int32, (tq, tk), 0)
            kpos = k0 + lax.broadcasted_iota(jnp.int32, (tq, tk), 1)
            s = jnp.where(kpos <= qpos, s, -jnp.inf)
        m_old = m_s[...]
        m_new = jnp.maximum(m_old, jnp.max(s, axis=-1, keepdims=True))
        alpha = jnp.exp(m_old - m_new)
        p = jnp.exp(s - m_new)
        l_s[...] = alpha * l_s[...] + jnp.sum(p, axis=-1, keepdims=True)
        acc_s[...] = alpha * acc_s[...] + _dot(p.astype(BF16), v_ref[0, 0, pl.ds(k0, tk), :])
        m_s[...] = m_new

    n_full = qi * (tq // tk)

    def full_step(j, c):
        update(j, False)
        return c

    lax.fori_loop(0, n_full, full_step, 0)
    for d in range(tq // tk):
        update(n_full + d, True)
    o_ref[0] = (acc_s[...] / l_s[...]).astype(BF16)


def _attn_prompt(q, kt, v, *, tq, tk):
    b, h, s, _ = q.shape
    return pl.pallas_call(
        functools.partial(_attn_prompt_kernel, tq=tq, tk=tk),
        grid=(b, h, s // tq),
        in_specs=[pl.BlockSpec((1, 1, tq, HEAD_PAD), lambda bi, hi, i: (bi, hi, i, 0)),
                  pl.BlockSpec((1, HEAD_PAD, s), lambda bi, hi, i: (bi, hi, 0)),
                  pl.BlockSpec((1, 1, s, HEAD_PAD), lambda bi, hi, i: (bi, hi, 0, 0))],
        out_specs=pl.BlockSpec((1, tq, HEAD_PAD), lambda bi, hi, i: (bi, i, hi)),
        out_shape=jax.ShapeDtypeStruct((b, s, h * HEAD_PAD), BF16),
        scratch_shapes=[pltpu.VMEM((tq, 1), F32), pltpu.VMEM((tq, 1), F32), pltpu.VMEM((tq, HEAD_PAD), F32)],
        compiler_params=_params("parallel", "parallel", "arbitrary"),
        name="attn_prompt",
    )(q, kt, v)


def _absorb_kernel(qn_ref, wukt_ref, qa_ref):
    qn = qn_ref[...]
    for h in range(N_HEADS):
        qa_ref[:, h * KV_LORA:(h + 1) * KV_LORA] = _dot(
            qn[:, h * QK_NOPE:(h + 1) * QK_NOPE], wukt_ref[h]).astype(BF16)


def _absorb(qn, wukt3):
    n = qn.shape[0]
    return pl.pallas_call(
        _absorb_kernel,
        out_shape=jax.ShapeDtypeStruct((n, N_HEADS * KV_LORA), BF16),
        compiler_params=pltpu.CompilerParams(vmem_limit_bytes=VMEM_LIMIT_BYTES),
        name="absorb",
    )(qn, wukt3)


def _decode_kernel(pt_ref, qa_ref, qr_ref, cn_ref, kn_ref, *refs, n_pp):
    del pt_ref
    ckv_refs, kr_refs = refs[:n_pp], refs[n_pp:2 * n_pp]
    o_ref, m_s, l_s, acc_s = refs[2 * n_pp:]
    j = pl.program_id(1)
    qa = qa_ref[0]
    qr = qr_ref[0]

    @pl.when(j == 0)
    def _():
        cn = cn_ref[0].astype(BF16).astype(F32)
        kn = kn_ref[0].astype(BF16).astype(F32)
        m_s[...] = (jnp.sum(qa.astype(F32) * cn, axis=-1, keepdims=True)
                    + jnp.sum(qr.astype(F32) * kn, axis=-1, keepdims=True))
        l_s[...] = jnp.ones(l_s.shape, F32)
        acc_s[...] = jnp.broadcast_to(cn, acc_s.shape)

    pages = [r[0].astype(BF16) for r in ckv_refs]
    s = jnp.concatenate(
        [_dot_nt(qa, pages[k]) + _dot_nt(qr, kr_refs[k][0].astype(BF16)) for k in range(n_pp)], axis=1)
    m_old = m_s[...]
    m_new = jnp.maximum(m_old, jnp.max(s, axis=-1, keepdims=True))
    alpha = jnp.exp(m_old - m_new)
    p = jnp.exp(s - m_new)
    l_s[...] = alpha * l_s[...] + jnp.sum(p, axis=-1, keepdims=True)
    pv = _dot(p[:, 0:PAGE_SIZE].astype(BF16), pages[0])
    for k in range(1, n_pp):
        pv = pv + _dot(p[:, k * PAGE_SIZE:(k + 1) * PAGE_SIZE].astype(BF16), pages[k])
    acc_s[...] = alpha * acc_s[...] + pv
    m_s[...] = m_new

    @pl.when(j == pl.num_programs(1) - 1)
    def _():
        o_ref[0] = (acc_s[...] / l_s[...]).astype(BF16)


def _decode(page_table, qa, qr, ckv_new, kr_new, pool_ckv, pool_kr, *, n_pp):
    n, n_pages = page_table.shape
    pt_flat = page_table.reshape(-1)

    def page_spec(d, k):
        return pl.BlockSpec((1, PAGE_SIZE, d), lambda bi, j, pt: (pt[bi * n_pages + j * n_pp + k], 0, 0))

    per_row = lambda d0, d1: pl.BlockSpec((1, d0, d1), lambda bi, j, pt: (bi, 0, 0))
    grid_spec = pltpu.PrefetchScalarGridSpec(
        num_scalar_prefetch=1,
        grid=(n, n_pages // n_pp),
        in_specs=([per_row(N_HEADS, KV_LORA), per_row(N_HEADS, QK_ROPE), per_row(1, KV_LORA), per_row(1, QK_ROPE)]
                  + [page_spec(KV_LORA, k) for k in range(n_pp)] + [page_spec(QK_ROPE, k) for k in range(n_pp)]),
        out_specs=per_row(N_HEADS, KV_LORA),
        scratch_shapes=[pltpu.VMEM((N_HEADS, 1), F32), pltpu.VMEM((N_HEADS, 1), F32),
                        pltpu.VMEM((N_HEADS, KV_LORA), F32)],
    )
    return pl.pallas_call(
        functools.partial(_decode_kernel, n_pp=n_pp),
        grid_spec=grid_spec,
        out_shape=jax.ShapeDtypeStruct((n, N_HEADS, KV_LORA), BF16),
        compiler_params=_params("parallel", "arbitrary"),
        name="decode",
    )(pt_flat, qa, qr, ckv_new, kr_new, *([pool_ckv] * n_pp), *([pool_kr] * n_pp))


def _unabsorb_kernel(ol_ref, wuv_ref, o_ref):
    for h in range(N_HEADS):
        o_ref[:, h * HEAD_PAD:(h + 1) * HEAD_PAD] = _dot(
            ol_ref[:, h * KV_LORA:(h + 1) * KV_LORA], wuv_ref[:, h * HEAD_PAD:(h + 1) * HEAD_PAD]).astype(BF16)


def _unabsorb(o_lat, w_uv_pad):
    n = o_lat.shape[0]
    return pl.pallas_call(
        _unabsorb_kernel,
        out_shape=jax.ShapeDtypeStruct((n, N_HEADS * HEAD_PAD), BF16),
        compiler_params=pltpu.CompilerParams(vmem_limit_bytes=VMEM_LIMIT_BYTES),
        name="unabsorb",
    )(o_lat, w_uv_pad)


def _merge_kernel(x_ref, hg_ref, o_ref, ga_ref, gb_ref, wrnn_ref, wattn_ref, wout_ref, post_ref, y_ref):
    y_a = _dot(hg_ref[...], wrnn_ref[...])
    y_b = _dot(o_ref[...], wattn_ref[...])
    m = ga_ref[...].astype(F32) * y_a + gb_ref[...].astype(F32) * y_b
    y_ref[...] = x_ref[...] + _rms(_dot(m.astype(BF16), wout_ref[...]), post_ref[...])


def _merge(x, hg, o, ga, gb, w, *, tm):
    n = x.shape[0]
    row = lambda d: pl.BlockSpec((tm, d), lambda i: (i, 0))
    hp = N_HEADS * HEAD_PAD
    return pl.pallas_call(
        _merge_kernel,
        grid=(n // tm,),
        in_specs=[row(D_MODEL), row(D_RNN), row(hp), row(D_MODEL), row(D_MODEL),
                  _resident((D_RNN, D_MODEL)), _resident((hp, D_MODEL)), _resident((D_MODEL, D_MODEL)),
                  _resident((1, D_MODEL))],
        out_specs=row(D_MODEL),
        out_shape=jax.ShapeDtypeStruct((n, D_MODEL), F32),
        compiler_params=_params("parallel"),
        name="merge",
    )(x, hg, o, ga, gb, w["w_branch_rnn"], w["w_attn_pad"], w["w_out"], w["mix_post"])


def _ple_kernel(x_ref, p_ref, wg_ref, wp_ref, post_ref, y_ref):
    x = x_ref[...]
    e = jax.nn.sigmoid(_dot(x.astype(BF16), wg_ref[...])) * _dot(p_ref[...].astype(BF16), wp_ref[...])
    y_ref[...] = x + _rms(e, post_ref[...])


def _ple(x, p, w, *, tm):
    n = x.shape[0]
    row = lambda d: pl.BlockSpec((tm, d), lambda i: (i, 0))
    return pl.pallas_call(
        _ple_kernel,
        grid=(n // tm,),
        in_specs=[row(D_MODEL), row(D_PLE), _resident((D_MODEL, D_MODEL)), _resident((D_PLE, D_MODEL)),
                  _resident((1, D_MODEL))],
        out_specs=row(D_MODEL),
        out_shape=jax.ShapeDtypeStruct((n, D_MODEL), F32),
        compiler_params=_params("parallel"),
        name="ple",
    )(x, p, w["ple_gate"], w["ple_proj"], w["ple_post"])


def _rot_cols(w):
    half = QK_ROPE // 2
    return jnp.concatenate([-w[..., half:], w[..., :half]], axis=-1)


def _pad_axis(a, axis, before, after):
    pads = [(0, 0)] * a.ndim
    pads[axis] = (before, after)
    return jnp.pad(a, pads)


def _prep_layer(i, ffn1_pre, ffn1_w_gate, ffn1_w_up, ffn1_w_down, ffn1_post, mix_pre, w_in, conv_w, conv_b,
                lru_w_a, lru_b_a, lru_w_i, lru_b_i, lru_lambda, w_branch_rnn, q_norm, w_uq, w_qr, kv_norm, w_uk,
                w_uv, w_branch_attn, w_out, mix_post, ffn2_pre, ffn2_w_gate, ffn2_w_up, ffn2_w_down, ffn2_post,
                ple_gate, ple_proj, ple_post):
    vec = lambda a: a[i].reshape(1, -1)
    w_in_i = w_in[i]
    w_kr = w_in_i[:, O_KV:O_KR]
    pad_r = HEAD_PAD - QK_ROPE - QK_NOPE
    w_qr_i, w_uq_i = w_qr[i], w_uq[i]
    zeros_n = jnp.zeros_like(w_uq_i)
    blockdiag = lambda wb: (jnp.eye(RNN_BLOCKS, dtype=F32)[:, None, :, None] * wb[:, :, None, :]).reshape(D_RNN, D_RNN)
    return {
        "ffn1": (vec(ffn1_pre), ffn1_w_gate[i].astype(BF16), ffn1_w_up[i].astype(BF16),
                 ffn1_w_down[i].astype(BF16), vec(ffn1_post)),
        "ffn2": (vec(ffn2_pre), ffn2_w_gate[i].astype(BF16), ffn2_w_up[i].astype(BF16),
                 ffn2_w_down[i].astype(BF16), vec(ffn2_post)),
        "mix_pre": vec(mix_pre),
        "w_main": jnp.concatenate([w_in_i[:, :O_KV], w_in_i[:, O_KR:]], axis=1).astype(BF16),
        "w_kr2": jnp.concatenate([w_kr, _rot_cols(w_kr)], axis=1).astype(BF16),
        "q_norm": vec(q_norm), "kv_norm": vec(kv_norm),
        "w_q": _pad_axis(jnp.concatenate([w_qr_i, w_uq_i], axis=-1), 2, 0, pad_r
                         ).reshape(Q_LORA, N_HEADS * HEAD_PAD).astype(BF16),
        "w_q_rot": _pad_axis(jnp.concatenate([_rot_cols(w_qr_i), zeros_n], axis=-1), 2, 0, pad_r
                             ).reshape(Q_LORA, N_HEADS * HEAD_PAD).astype(BF16),
        "w_ukt": _pad_axis(jnp.transpose(w_uk[i], (1, 2, 0)), 1, QK_ROPE, pad_r
                           ).reshape(N_HEADS * HEAD_PAD, KV_LORA).astype(BF16),
        "w_krt2": jnp.concatenate([_pad_axis(w_kr.T, 0, 0, HEAD_PAD - QK_ROPE),
                                   _pad_axis(_rot_cols(w_kr).T, 0, 0, HEAD_PAD - QK_ROPE)], axis=0).astype(BF16),
        "w_uv_pad": _pad_axis(w_uv[i], 2, 0, HEAD_PAD - V_HEAD).reshape(KV_LORA, N_HEADS * HEAD_PAD).astype(BF16),
        "w_uq": w_uq_i.reshape(Q_LORA, N_HEADS * QK_NOPE).astype(BF16),
        "w_qr": w_qr_i.reshape(Q_LORA, N_HEADS * QK_ROPE).astype(BF16),
        "w_qr_rot": _rot_cols(w_qr_i).reshape(Q_LORA, N_HEADS * QK_ROPE).astype(BF16),
        "w_ukt3": jnp.transpose(w_uk[i], (1, 2, 0)).astype(BF16),
        "conv_w": conv_w[i], "conv_b": vec(conv_b),
        "w_lru": jnp.concatenate([blockdiag(lru_w_a[i]), blockdiag(lru_w_i[i])], axis=1).astype(BF16),
        "b_lru": jnp.concatenate([lru_b_a[i].reshape(1, -1), lru_b_i[i].reshape(1, -1)], axis=1),
        "lam": vec(lru_lambda),
        "w_branch_rnn": w_branch_rnn[i].astype(BF16),
        "w_attn_pad": _pad_axis(w_branch_attn[i].reshape(N_HEADS, V_HEAD, D_MODEL), 1, 0, HEAD_PAD - V_HEAD
                                ).reshape(N_HEADS * HEAD_PAD, D_MODEL).astype(BF16),
        "w_out": w_out[i].astype(BF16), "mix_post": vec(mix_post),
        "ple_gate": ple_gate[i].astype(BF16), "ple_proj": ple_proj[i].astype(BF16), "ple_post": vec(ple_post),
    }


def _rope_tables(pos, *, prompt):
    half = QK_ROPE // 2
    freqs = ROPE_THETA ** (-jnp.arange(half, dtype=F32) / half)
    ang = pos.astype(F32)[:, None] * freqs[None, :]
    cos32 = jnp.tile(jnp.cos(ang), (1, 2))
    sin32 = jnp.tile(jnp.sin(ang), (1, 2))
    n = pos.shape[0]
    tabs = {"cos32": cos32, "sin32": sin32}
    if prompt:
        tabs["cos128"] = SM_SCALE * jnp.concatenate(
            [cos32, jnp.ones((n, QK_NOPE), F32), jnp.zeros((n, HEAD_PAD - QK_ROPE - QK_NOPE), F32)], axis=1)
        tabs["sin128"] = SM_SCALE * _pad_axis(sin32, 1, 0, HEAD_PAD - QK_ROPE)
        tabs["cos_t"] = _pad_axis(cos32.T, 0, 0, HEAD_PAD - QK_ROPE)
        tabs["sin_t"] = _pad_axis(sin32.T, 0, 0, HEAD_PAD - QK_ROPE)
    else:
        tabs["cos512"] = SM_SCALE * jnp.tile(cos32, (1, N_HEADS))
        tabs["sin512"] = SM_SCALE * jnp.tile(sin32, (1, N_HEADS))
    return tabs


def _tile(n, pref):
    return pref if n % pref == 0 else n


def _layer_prompt(x, p, w):
    b, s, _ = x.shape
    n = b * s
    tm = _tile(n, 512)
    tabs = _rope_tables(jnp.arange(s, dtype=jnp.int32), prompt=True)
    x1 = _ffn(x.reshape(n, D_MODEL), *w["ffn1"], tm=tm)
    xr, gy, ckv, krope, ga, gb, q, kt, v = _proj(x1.reshape(b, s, D_MODEL), w, tabs, prompt=True, tm=_tile(s, 256))
    hg, h_last, conv_new = _rglru_prompt(xr, gy, w, ts=_tile(s, 256))
    o = _attn_prompt(q, kt, v, tq=_tile(s, 512), tk=_tile(s, 512))
    x2 = _merge(x1, hg.reshape(n, D_RNN), o.reshape(n, N_HEADS * HEAD_PAD), ga.reshape(n, D_MODEL),
                gb.reshape(n, D_MODEL), w, tm=tm)
    x3 = _ffn(x2, *w["ffn2"], tm=tm)
    y = _ple(x3, p.reshape(n, D_PLE), w, tm=tm)
    return y.reshape(b, s, D_MODEL), (ckv, krope, h_last.reshape(b, D_RNN), conv_new)


def _layer_sample(x, p, h0, conv_buf, pool_ckv, pool_kr, page_table, w):
    n, s, _ = x.shape
    past_len = page_table.shape[1] * PAGE_SIZE
    tabs = _rope_tables(jnp.full((n,), past_len, jnp.int32), prompt=False)
    x1 = _ffn(x.reshape(n, D_MODEL), *w["ffn1"], tm=n)
    xr, gy, ckv, krope, ga, gb, qn, qr = _proj(x1.reshape(1, n, D_MODEL), w, tabs, prompt=False, tm=n)
    hg, h_new, conv_new = _rglru_sample(xr[0], gy[0], jnp.transpose(conv_buf, (1, 0, 2)), h0, w)
    qa = _absorb(qn[0], w["w_ukt3"]).reshape(n, N_HEADS, KV_LORA)
    n_pages = page_table.shape[1]
    o_lat = _decode(page_table, qa, qr.reshape(n, N_HEADS, QK_ROPE), ckv.reshape(n, 1, KV_LORA),
                    krope.reshape(n, 1, QK_ROPE), pool_ckv, pool_kr, n_pp=16 if n_pages % 16 == 0 else 1)
    o = _unabsorb(o_lat.reshape(n, N_HEADS * KV_LORA), w["w_uv_pad"])
    x2 = _merge(x1, hg, o, ga[0], gb[0], w, tm=n)
    x3 = _ffn(x2, *w["ffn2"], tm=n)
    y = _ple(x3, p.reshape(n, D_PLE), w, tm=n)
    return (y.reshape(n, s, D_MODEL),
            (ckv.reshape(n, s, KV_LORA), krope.reshape(n, s, QK_ROPE), h_new, jnp.transpose(conv_new, (1, 0, 2))))


def kernel(x_prompt, x_sample, p_prompt, p_sample, cache_ckv, cache_krope, state_h, state_conv, page_table,
           ffn1_pre, ffn1_w_gate, ffn1_w_up, ffn1_w_down, ffn1_post, mix_pre, w_in, conv_w, conv_b, lru_w_a,
           lru_b_a, lru_w_i, lru_b_i, lru_lambda, w_branch_rnn, q_norm, w_uq, w_qr, kv_norm, w_uk, w_uv,
           w_branch_attn, w_out, mix_post, ffn2_pre, ffn2_w_gate, ffn2_w_up, ffn2_w_down, ffn2_post, ple_gate,
           ple_proj, ple_post):
    assert x_sample.shape[1] == 1, "the sample group carries one new token per sequence"
    depth = ffn1_pre.shape[0]
    hp, hs = x_prompt, x_sample
    st_p, st_s = [], []
    for i in range(depth):
        w = _prep_layer(i, ffn1_pre, ffn1_w_gate, ffn1_w_up, ffn1_w_down, ffn1_post, mix_pre, w_in, conv_w, conv_b,
                        lru_w_a, lru_b_a, lru_w_i, lru_b_i, lru_lambda, w_branch_rnn, q_norm, w_uq, w_qr, kv_norm,
                        w_uk, w_uv, w_branch_attn, w_out, mix_post, ffn2_pre, ffn2_w_gate, ffn2_w_up, ffn2_w_down,
                        ffn2_post, ple_gate, ple_proj, ple_post)
        hp, sp = _layer_prompt(hp, p_prompt[i], w)
        hs, ss = _layer_sample(hs, p_sample[i], state_h[i], state_conv[i], cache_ckv[i], cache_krope[i],
                               page_table, w)
        st_p.append(sp)
        st_s.append(ss)
    stack = lambda sts, k: jnp.stack([s[k] for s in sts])
    return (hp, hs, stack(st_p, 0), stack(st_p, 1), stack(st_p, 2), stack(st_p, 3),
            stack(st_s, 0), stack(st_s, 1), stack(st_s, 2), stack(st_s, 3))
```

```python
import functools

import jax
import jax.numpy as jnp
from jax import lax
from jax.experimental import pallas as pl
from jax.experimental.pallas import tpu as pltpu

D_MODEL = 1024
D_RNN = 1280
RNN_BLOCKS = 16
RNN_BLOCK = D_RNN // RNN_BLOCKS
CONV_W = 4
LRU_C = 8.0
N_HEADS = 16
QK_NOPE = 64
QK_ROPE = 32
V_HEAD = 64
Q_LORA = 384
KV_LORA = 256
ROPE_THETA = 10000.0
SM_SCALE = (QK_NOPE + QK_ROPE) ** -0.5
Q_SCALE = SM_SCALE * 1.4426950408889634
D_FF = 2816
D_PLE = 256
EPS = 1e-6
PAGE_SIZE = 128
O_XR = D_RNN
O_YR = O_XR + D_RNN
O_Q = O_YR + Q_LORA
O_KV = O_Q + KV_LORA
O_KR = O_KV + QK_ROPE

LANES = 128
HEAD_PAD = LANES
VMEM_LIMIT_BYTES = 56 * 1024 * 1024

F32 = jnp.float32
BF16 = jnp.bfloat16


def _rms(x, g):
    return x * lax.rsqrt(jnp.mean(x * x, axis=-1, keepdims=True) + EPS) * g


def _dot(a, b):
    return jnp.dot(a, b, preferred_element_type=F32)


def _dot_nt(a, b):
    return lax.dot_general(a, b, (((1,), (1,)), ((), ())), preferred_element_type=F32)


def _resident(shape):
    nd = len(shape)
    return pl.BlockSpec(shape, lambda *_: (0,) * nd, pipeline_mode=pl.Buffered(1))


def _params(*sem):
    return pltpu.CompilerParams(dimension_semantics=sem, vmem_limit_bytes=VMEM_LIMIT_BYTES)


def _ffn_kernel(x_ref, pre_ref, wg_ref, wu_ref, wd_ref, post_ref, o_ref, *, f_chunk):
    x = x_ref[...]
    u = _rms(x, pre_ref[...]).astype(BF16)
    acc = jnp.zeros(x.shape, F32)
    for c in range(D_FF // f_chunk):
        sl = slice(c * f_chunk, (c + 1) * f_chunk)
        g = _dot(u, wg_ref[:, sl])
        h = (g * jax.nn.sigmoid(g)) * _dot(u, wu_ref[:, sl])
        acc = acc + _dot(h.astype(BF16), wd_ref[sl, :])
    o_ref[...] = x + 0.5 * _rms(acc, post_ref[...])


def _ffn(x, pre, wg, wu, wd, post, *, tm):
    n = x.shape[0]
    return pl.pallas_call(
        functools.partial(_ffn_kernel, f_chunk=D_FF // 2),
        grid=(n // tm,),
        in_specs=[pl.BlockSpec((tm, D_MODEL), lambda i: (i, 0)),
                  _resident((1, D_MODEL)), _resident((D_MODEL, D_FF)), _resident((D_MODEL, D_FF)),
                  _resident((D_FF, D_MODEL)), _resident((1, D_MODEL))],
        out_specs=pl.BlockSpec((tm, D_MODEL), lambda i: (i, 0)),
        out_shape=jax.ShapeDtypeStruct((n, D_MODEL), F32),
        compiler_params=_params("parallel"),
        name="ffn",
    )(x, pre, wg, wu, wd, post)


N_MAIN = 2 * D_RNN + Q_LORA + KV_LORA + 2 * D_MODEL
M_YR = D_RNN
M_CQ = 2 * D_RNN
M_KV = M_CQ + Q_LORA
M_GA = M_KV + KV_LORA
M_GB = M_GA + D_MODEL


def _proj_kernel(*refs, prompt):
    (x_ref, pre_ref, wmain_ref, qn_ref, kvn_ref, wkr_ref, cosk_ref, sink_ref), refs = refs[:8], refs[8:]
    if prompt:
        (wqt_ref, wqtrot_ref, cosqt_ref, sinqt_ref, wuk_ref, wuvt_ref), refs = refs[:6], refs[6:]
    else:
        (wuq_ref, wqr_ref, wqrrot_ref, cos512_ref, sin512_ref), refs = refs[:5], refs[5:]
    xr_ref, gy_ref, ckv_ref, kr_ref, ga_ref, gb_ref = refs[:6]

    u = _rms(x_ref[0], pre_ref[...]).astype(BF16)
    xr_ref[0] = _dot(u, wmain_ref[:, 0:M_YR])
    gy_ref[0] = jax.nn.gelu(_dot(u, wmain_ref[:, M_YR:M_CQ])).astype(BF16)
    ga_ref[0] = jax.nn.sigmoid(_dot(u, wmain_ref[:, M_GA:M_GB])).astype(BF16)
    gb_ref[0] = jax.nn.sigmoid(_dot(u, wmain_ref[:, M_GB:N_MAIN])).astype(BF16)
    cq = _rms(_dot(u, wmain_ref[:, M_CQ:M_KV]), qn_ref[...]).astype(BF16)
    ckv = _rms(_dot(u, wmain_ref[:, M_KV:M_GA]), kvn_ref[...])
    ckv_ref[0] = ckv
    ckv_b = ckv.astype(BF16)
    kr2 = _dot(u, wkr_ref[...])
    kr = kr2[:, :HEAD_PAD] * cosk_ref[...] + kr2[:, HEAD_PAD:] * sink_ref[...]
    kr_ref[0] = kr[:, :QK_ROPE]

    if prompt:
        qt_ref, k_ref, vt_ref = refs[6:]
        k_nope = _dot(ckv_b, wuk_ref[...])
        qa_t = _dot_nt(wqt_ref[...], cq)
        qb_t = _dot_nt(wqtrot_ref[...], cq)
        cos_t = cosqt_ref[...]
        sin_t = sinqt_ref[...]
        for h in range(N_HEADS):
            sl = slice(h * HEAD_PAD, (h + 1) * HEAD_PAD)
            k_ref[0, h] = (k_nope[:, sl] + kr).astype(BF16)
            qt_ref[0, sl, :] = (qa_t[sl, :] * cos_t + qb_t[sl, :] * sin_t).astype(BF16)
        v_t = _dot_nt(wuvt_ref[...], ckv_b)
        head_row = lax.broadcasted_iota(jnp.int32, v_t.shape, 0) & (HEAD_PAD - 1)
        vt_ref[0] = jnp.where(head_row == V_HEAD, 1.0, v_t).astype(BF16)
    else:
        qn_out_ref, qr_out_ref = refs[6:]
        qn_out_ref[0] = (_dot(cq, wuq_ref[...]) * Q_SCALE).astype(BF16)
        qr_out_ref[0] = (_dot(cq, wqr_ref[...]) * cos512_ref[...]
                         + _dot(cq, wqrrot_ref[...]) * sin512_ref[...]).astype(BF16)


def _proj(x, w, tabs, *, prompt, tm):
    b, s, _ = x.shape
    row = lambda d: pl.BlockSpec((1, tm, d), lambda bi, i: (bi, i, 0))
    tab = lambda d: pl.BlockSpec((tm, d), lambda bi, i: (i, 0))
    tab_t = pl.BlockSpec((HEAD_PAD, tm), lambda bi, i: (0, i))
    in_specs = [row(D_MODEL), _resident((1, D_MODEL)), _resident((D_MODEL, N_MAIN)), _resident((1, Q_LORA)),
                _resident((1, KV_LORA)), _resident((D_MODEL, 2 * HEAD_PAD)), tab(HEAD_PAD), tab(HEAD_PAD)]
    args = [x, w["mix_pre"], w["w_main"], w["q_norm"], w["kv_norm"], w["w_kr2"], tabs["cos_k"], tabs["sin_k"]]
    out_specs = [row(D_RNN), row(D_RNN), row(KV_LORA), row(QK_ROPE), row(D_MODEL), row(D_MODEL)]
    out_shape = [jax.ShapeDtypeStruct((b, s, D_RNN), F32), jax.ShapeDtypeStruct((b, s, D_RNN), BF16),
                 jax.ShapeDtypeStruct((b, s, KV_LORA), F32), jax.ShapeDtypeStruct((b, s, QK_ROPE), F32),
                 jax.ShapeDtypeStruct((b, s, D_MODEL), BF16), jax.ShapeDtypeStruct((b, s, D_MODEL), BF16)]
    hp = N_HEADS * HEAD_PAD
    if prompt:
        in_specs += [_resident((hp, Q_LORA)), _resident((hp, Q_LORA)), tab_t, tab_t,
                     _resident((KV_LORA, hp)), _resident((hp, KV_LORA))]
        args += [w["w_qt"], w["w_qt_rot"], tabs["cos_qt"], tabs["sin_qt"], w["w_uk_pad"], w["w_uvt_pad"]]
        lanes_major = pl.BlockSpec((1, hp, tm), lambda bi, i: (bi, 0, i))
        out_specs += [lanes_major, pl.BlockSpec((1, N_HEADS, tm, HEAD_PAD), lambda bi, i: (bi, 0, i, 0)), lanes_major]
        out_shape += [jax.ShapeDtypeStruct((b, hp, s), BF16),
                      jax.ShapeDtypeStruct((b, N_HEADS, s, HEAD_PAD), BF16),
                      jax.ShapeDtypeStruct((b, hp, s), BF16)]
    else:
        nn, nr = N_HEADS * QK_NOPE, N_HEADS * QK_ROPE
        in_specs += [_resident((Q_LORA, nn)), _resident((Q_LORA, nr)), _resident((Q_LORA, nr)), tab(nr), tab(nr)]
        args += [w["w_uq"], w["w_qr"], w["w_qr_rot"], tabs["cos512"], tabs["sin512"]]
        out_specs += [row(nn), row(nr)]
        out_shape += [jax.ShapeDtypeStruct((b, s, nn), BF16), jax.ShapeDtypeStruct((b, s, nr), BF16)]
    return pl.pallas_call(
        functools.partial(_proj_kernel, prompt=prompt),
        grid=(b, s // tm),
        in_specs=in_specs, out_specs=out_specs, out_shape=out_shape,
        compiler_params=_params("parallel", "parallel"),
        name="proj_prompt" if prompt else "proj_sample",
    )(*args)


def _softplus(x):
    return jnp.maximum(x, 0.0) + jnp.log1p(jnp.exp(-jnp.abs(x)))


def _lru_coeffs(xc, wbd_ref, bab_ref, lam_ref):
    z = _dot(xc.astype(BF16), wbd_ref[...]) + bab_ref[...]
    r = jax.nn.sigmoid(z[:, :D_RNN])
    gi = jax.nn.sigmoid(z[:, D_RNN:])
    log_a = -LRU_C * r * _softplus(-lam_ref[...])
    a = jnp.exp(log_a)
    th = jnp.tanh(log_a)
    mult = jnp.sqrt(-2.0 * th / (1.0 - th))
    return a, mult * (gi * xc)


def _rglru_prompt_kernel(xr_ref, gy_ref, cw_ref, cb_ref, wbd_ref, bab_ref, lam_ref,
                         hg_ref, hlast_ref, convnew_ref, xbuf, a_s, b_s, h_s, hcar, *, ts):
    t = pl.program_id(1)
    halo = 8

    @pl.when(t == 0)
    def _():
        xbuf[0:halo, :] = jnp.zeros((halo, D_RNN), F32)
        hcar[...] = jnp.zeros((1, D_RNN), F32)

    x = xr_ref[0]
    xbuf[halo:halo + ts, :] = x
    xc = cb_ref[...] + xbuf[halo - 3:halo - 3 + ts, :] * cw_ref[0:1, :]
    xc = xc + xbuf[halo - 2:halo - 2 + ts, :] * cw_ref[1:2, :]
    xc = xc + xbuf[halo - 1:halo - 1 + ts, :] * cw_ref[2:3, :]
    xc = xc + x * cw_ref[3:4, :]
    xbuf[0:halo, :] = x[ts - halo:, :]

    a, b = _lru_coeffs(xc, wbd_ref, bab_ref, lam_ref)
    a_s[...] = a
    b_s[...] = b

    def step(i, h):
        h = a_s[pl.ds(i, 1), :] * h + b_s[pl.ds(i, 1), :]
        h_s[pl.ds(i, 1), :] = h
        return h

    h = lax.fori_loop(0, ts, step, hcar[...], unroll=8)
    hcar[...] = h
    hg_ref[0] = (h_s[...] * gy_ref[0].astype(F32)).astype(BF16)

    @pl.when(t == pl.num_programs(1) - 1)
    def _():
        hlast_ref[0] = h
        convnew_ref[0] = x[ts - (CONV_W - 1):, :]


def _rglru_prompt(xr, gy, w, *, ts):
    b, s, _ = xr.shape
    row = pl.BlockSpec((1, ts, D_RNN), lambda bi, t: (bi, t, 0))
    return pl.pallas_call(
        functools.partial(_rglru_prompt_kernel, ts=ts),
        grid=(b, s // ts),
        in_specs=[row, row, _resident((CONV_W, D_RNN)), _resident((1, D_RNN)), _resident((D_RNN, 2 * D_RNN)),
                  _resident((1, 2 * D_RNN)), _resident((1, D_RNN))],
        out_specs=[row, pl.BlockSpec((1, 1, D_RNN), lambda bi, t: (bi, 0, 0)),
                   pl.BlockSpec((1, CONV_W - 1, D_RNN), lambda bi, t: (bi, 0, 0))],
        out_shape=[jax.ShapeDtypeStruct((b, s, D_RNN), BF16), jax.ShapeDtypeStruct((b, 1, D_RNN), F32),
                   jax.ShapeDtypeStruct((b, CONV_W - 1, D_RNN), F32)],
        scratch_shapes=[pltpu.VMEM((ts + 8, D_RNN), F32), pltpu.VMEM((ts, D_RNN), F32),
                        pltpu.VMEM((ts, D_RNN), F32), pltpu.VMEM((ts, D_RNN), F32), pltpu.VMEM((1, D_RNN), F32)],
        compiler_params=_params("parallel", "arbitrary"),
        name="rglru_prompt",
    )(xr, gy, w["conv_w"], w["conv_b"], w["w_lru"], w["b_lru"], w["lam"])


def _rglru_sample_kernel(xr_ref, gy_ref, sc_ref, h0_ref, cw_ref, cb_ref, wbd_ref, bab_ref, lam_ref,
                         hg_ref, hnew_ref, convnew_ref):
    x = xr_ref[...]
    xc = cb_ref[...] + sc_ref[0] * cw_ref[0:1, :]
    xc = xc + sc_ref[1] * cw_ref[1:2, :]
    xc = xc + sc_ref[2] * cw_ref[2:3, :]
    xc = xc + x * cw_ref[3:4, :]
    a, b = _lru_coeffs(xc, wbd_ref, bab_ref, lam_ref)
    h = a * h0_ref[...] + b
    hnew_ref[...] = h
    hg_ref[...] = (h * gy_ref[...].astype(F32)).astype(BF16)
    convnew_ref[0] = sc_ref[1]
    convnew_ref[1] = sc_ref[2]
    convnew_ref[2] = x


def _rglru_sample(xr, gy, sc, h0, w):
    n = xr.shape[0]
    return pl.pallas_call(
        _rglru_sample_kernel,
        out_shape=[jax.ShapeDtypeStruct((n, D_RNN), BF16), jax.ShapeDtypeStruct((n, D_RNN), F32),
                   jax.ShapeDtypeStruct((CONV_W - 1, n, D_RNN), F32)],
        compiler_params=pltpu.CompilerParams(vmem_limit_bytes=VMEM_LIMIT_BYTES),
        name="rglru_sample",
    )(xr, gy, sc, h0, w["conv_w"], w["conv_b"], w["w_lru"], w["b_lru"], w["lam"])


HEADS_PER_STEP = 2


def _attn_prompt_kernel(qt_ref, k_ref, vt_ref, o_ref, m_s, acc_s, s_a, s_b, *, t):
    qi = pl.program_id(2)
    heads = range(HEADS_PER_STEP)
    m_s[...] = jnp.full(m_s.shape, -jnp.inf, F32)
    acc_s[...] = jnp.zeros(acc_s.shape, F32)

    def rows(hh):
        return slice(hh * HEAD_PAD, (hh + 1) * HEAD_PAD)

    def scores(j, hh):
        k0 = pl.multiple_of(j * t, t)
        return _dot(k_ref[0, hh, pl.ds(k0, t), :], qt_ref[0, rows(hh), :])

    def consume(s, j, hh, masked):
        k0 = pl.multiple_of(j * t, t)
        if masked:
            kpos = lax.broadcasted_iota(jnp.int32, (t, t), 0)
            qpos = lax.broadcasted_iota(jnp.int32, (t, t), 1)
            s = jnp.where(kpos <= qpos, s, -jnp.inf)
        m_old = m_s[hh]
        m_new = jnp.maximum(m_old, jnp.max(s, axis=0, keepdims=True))
        p = jnp.exp2(s - m_new).astype(BF16)
        acc_s[hh] = jnp.exp2(m_old - m_new) * acc_s[hh] + _dot(vt_ref[0, rows(hh), pl.ds(k0, t)], p)
        m_s[hh] = m_new

    for hh in heads:
        s_a[hh] = scores(0, hh)

    def full_step(j, src, dst):
        for hh in heads:
            dst[hh] = scores(j + 1, hh)
            consume(src[hh], j, hh, False)

    def two_steps(i, c):
        full_step(2 * i, s_a, s_b)
        full_step(2 * i + 1, s_b, s_a)
        return c

    def diagonal_step(src):
        for hh in heads:
            consume(src[hh], qi, hh, True)
            acc = acc_s[hh]
            o_t = acc * (1.0 / acc[V_HEAD:V_HEAD + 1, :])
            o_ref[0, :, rows(hh)] = o_t.T.astype(BF16)

    lax.fori_loop(0, qi // 2, two_steps, 0)

    @pl.when(qi % 2 == 1)
    def _():
        full_step(qi - 1, s_a, s_b)
        diagonal_step(s_b)

    @pl.when(qi % 2 == 0)
    def _():
        diagonal_step(s_a)


def _attn_prompt(qt, k, vt, *, t):
    b, h, s, _ = k.shape
    hps = HEADS_PER_STEP
    return pl.pallas_call(
        functools.partial(_attn_prompt_kernel, t=t),
        grid=(b, h // hps, s // t),
        in_specs=[pl.BlockSpec((1, hps * HEAD_PAD, t), lambda bi, hi, i: (bi, hi, i)),
                  pl.BlockSpec((1, hps, s, HEAD_PAD), lambda bi, hi, i: (bi, hi, 0, 0)),
                  pl.BlockSpec((1, hps * HEAD_PAD, s), lambda bi, hi, i: (bi, hi, 0))],
        out_specs=pl.BlockSpec((1, t, hps * HEAD_PAD), lambda bi, hi, i: (bi, i, hi)),
        out_shape=jax.ShapeDtypeStruct((b, s, h * HEAD_PAD), BF16),
        scratch_shapes=[pltpu.VMEM((hps, 1, t), F32), pltpu.VMEM((hps, HEAD_PAD, t), F32),
                        pltpu.VMEM((hps, t, t), F32), pltpu.VMEM((hps, t, t), F32)],
        compiler_params=_params("parallel", "parallel", "arbitrary"),
        name="attn_prompt",
    )(qt, k, vt)


def _absorb_kernel(qn_ref, wukt_ref, qa_ref):
    qn = qn_ref[...]
    for h in range(N_HEADS):
        qa_ref[:, h * KV_LORA:(h + 1) * KV_LORA] = _dot(
            qn[:, h * QK_NOPE:(h + 1) * QK_NOPE], wukt_ref[h]).astype(BF16)


def _absorb(qn, wukt3):
    n = qn.shape[0]
    return pl.pallas_call(
        _absorb_kernel,
        out_shape=jax.ShapeDtypeStruct((n, N_HEADS * KV_LORA), BF16),
        compiler_params=pltpu.CompilerParams(vmem_limit_bytes=VMEM_LIMIT_BYTES),
        name="absorb",
    )(qn, wukt3)


def _decode_kernel(pt_ref, qa_ref, qr_ref, cn_ref, kn_ref, pool_ckv, pool_krt, o_ref,
                   ckv_buf, krt_buf, s_buf, p_buf, sems, *, n_pages):
    b = pl.program_id(0)
    slot = lax.rem(b, 2)

    def page_copies(page, sl, p):
        return (pltpu.make_async_copy(pool_ckv.at[page], ckv_buf.at[sl, p], sems.at[sl, 0]),
                pltpu.make_async_copy(pool_krt.at[page], krt_buf.at[sl, p], sems.at[sl, 1]))

    def start_page(row, sl, p):
        for cp in page_copies(pt_ref[row * n_pages + p], sl, p):
            cp.start()

    @pl.when(b == 0)
    def _():
        lax.fori_loop(0, n_pages, lambda p, c: (start_page(0, 0, p), c)[1], 0)

    for p in range(n_pages):
        for cp in page_copies(0, slot, p):
            cp.wait()

    qa = qa_ref[0]
    qr = qr_ref[0]

    def score_pages(prefetch):
        def score(p, c):
            if prefetch:
                start_page(b + 1, 1 - slot, p)
            page = ckv_buf[slot, p].astype(BF16)
            s = _dot_nt(qa, page) + _dot(qr, krt_buf[slot, p].astype(BF16))
            s_buf[:, pl.ds(pl.multiple_of(p * PAGE_SIZE, PAGE_SIZE), PAGE_SIZE)] = s
            return c
        lax.fori_loop(0, n_pages, score, 0, unroll=8)

    has_next = b + 1 < pl.num_programs(0)
    pl.when(has_next)(lambda: score_pages(True))
    pl.when(jnp.logical_not(has_next))(lambda: score_pages(False))

    cn = cn_ref[0].astype(BF16).astype(F32)
    kn = kn_ref[0].astype(BF16).astype(F32)
    s_new = (jnp.sum(qa.astype(F32) * cn, axis=-1, keepdims=True)
             + jnp.sum(qr.astype(F32) * kn, axis=-1, keepdims=True))
    s = s_buf[...]
    m = jnp.maximum(jnp.max(s, axis=-1, keepdims=True), s_new)
    p = jnp.exp2(s - m)
    p_new = jnp.exp2(s_new - m)
    l = jnp.sum(p, axis=-1, keepdims=True) + p_new
    p_buf[...] = p.astype(BF16)

    def attend(pi, acc):
        cols = pl.ds(pl.multiple_of(pi * PAGE_SIZE, PAGE_SIZE), PAGE_SIZE)
        return acc + _dot(p_buf[:, cols], ckv_buf[slot, pi].astype(BF16))

    acc = lax.fori_loop(0, n_pages, attend, p_new * cn, unroll=8)
    o_ref[0] = (acc / l).astype(BF16)


def _decode(page_table, qa, qr, ckv_new, kr_new, pool_ckv, pool_krt):
    n, n_pages = page_table.shape
    per_row = lambda d0, d1: pl.BlockSpec((1, d0, d1), lambda bi, pt: (bi, 0, 0))
    hbm = pl.BlockSpec(memory_space=pl.ANY)
    grid_spec = pltpu.PrefetchScalarGridSpec(
        num_scalar_prefetch=1,
        grid=(n,),
        in_specs=[per_row(N_HEADS, KV_LORA), per_row(N_HEADS, QK_ROPE), per_row(1, KV_LORA), per_row(1, QK_ROPE),
                  hbm, hbm],
        out_specs=per_row(N_HEADS, KV_LORA),
        scratch_shapes=[pltpu.VMEM((2, n_pages, PAGE_SIZE, KV_LORA), F32),
                        pltpu.VMEM((2, n_pages, QK_ROPE, PAGE_SIZE), F32),
                        pltpu.VMEM((N_HEADS, n_pages * PAGE_SIZE), F32),
                        pltpu.VMEM((N_HEADS, n_pages * PAGE_SIZE), BF16),
                        pltpu.SemaphoreType.DMA((2, 2))],
    )
    return pl.pallas_call(
        functools.partial(_decode_kernel, n_pages=n_pages),
        grid_spec=grid_spec,
        out_shape=jax.ShapeDtypeStruct((n, N_HEADS, KV_LORA), BF16),
        compiler_params=_params("arbitrary"),
        name="decode",
    )(page_table.reshape(-1), qa, qr, ckv_new, kr_new, pool_ckv, pool_krt)


def _unabsorb_kernel(ol_ref, wuv_ref, o_ref):
    for h in range(N_HEADS):
        o_ref[:, h * HEAD_PAD:(h + 1) * HEAD_PAD] = _dot(
            ol_ref[:, h * KV_LORA:(h + 1) * KV_LORA], wuv_ref[:, h * HEAD_PAD:(h + 1) * HEAD_PAD]).astype(BF16)


def _unabsorb(o_lat, w_uv_pad):
    n = o_lat.shape[0]
    return pl.pallas_call(
        _unabsorb_kernel,
        out_shape=jax.ShapeDtypeStruct((n, N_HEADS * HEAD_PAD), BF16),
        compiler_params=pltpu.CompilerParams(vmem_limit_bytes=VMEM_LIMIT_BYTES),
        name="unabsorb",
    )(o_lat, w_uv_pad)


def _merge_kernel(x_ref, hg_ref, o_ref, ga_ref, gb_ref, wrnn_ref, wattn_ref, wout_ref, post_ref, y_ref):
    y_a = _dot(hg_ref[...], wrnn_ref[...])
    y_b = _dot(o_ref[...], wattn_ref[...])
    m = ga_ref[...].astype(F32) * y_a + gb_ref[...].astype(F32) * y_b
    y_ref[...] = x_ref[...] + _rms(_dot(m.astype(BF16), wout_ref[...]), post_ref[...])


def _merge(x, hg, o, ga, gb, w, *, tm):
    n = x.shape[0]
    row = lambda d: pl.BlockSpec((tm, d), lambda i: (i, 0))
    hp = N_HEADS * HEAD_PAD
    return pl.pallas_call(
        _merge_kernel,
        grid=(n // tm,),
        in_specs=[row(D_MODEL), row(D_RNN), row(hp), row(D_MODEL), row(D_MODEL),
                  _resident((D_RNN, D_MODEL)), _resident((hp, D_MODEL)), _resident((D_MODEL, D_MODEL)),
                  _resident((1, D_MODEL))],
        out_specs=row(D_MODEL),
        out_shape=jax.ShapeDtypeStruct((n, D_MODEL), F32),
        compiler_params=_params("parallel"),
        name="merge",
    )(x, hg, o, ga, gb, w["w_branch_rnn"], w["w_attn_pad"], w["w_out"], w["mix_post"])


def _ple_kernel(x_ref, p_ref, wg_ref, wp_ref, post_ref, y_ref):
    x = x_ref[...]
    e = jax.nn.sigmoid(_dot(x.astype(BF16), wg_ref[...])) * _dot(p_ref[...].astype(BF16), wp_ref[...])
    y_ref[...] = x + _rms(e, post_ref[...])


def _ple(x, p, w, *, tm):
    n = x.shape[0]
    row = lambda d: pl.BlockSpec((tm, d), lambda i: (i, 0))
    return pl.pallas_call(
        _ple_kernel,
        grid=(n // tm,),
        in_specs=[row(D_MODEL), row(D_PLE), _resident((D_MODEL, D_MODEL)), _resident((D_PLE, D_MODEL)),
                  _resident((1, D_MODEL))],
        out_specs=row(D_MODEL),
        out_shape=jax.ShapeDtypeStruct((n, D_MODEL), F32),
        compiler_params=_params("parallel"),
        name="ple",
    )(x, p, w["ple_gate"], w["ple_proj"], w["ple_post"])


def _rot_cols(w):
    half = QK_ROPE // 2
    return jnp.concatenate([-w[..., half:], w[..., :half]], axis=-1)


def _pad_axis(a, axis, before, after):
    pads = [(0, 0)] * a.ndim
    pads[axis] = (before, after)
    return jnp.pad(a, pads)


def _prep_layer(i, ffn1_pre, ffn1_w_gate, ffn1_w_up, ffn1_w_down, ffn1_post, mix_pre, w_in, conv_w, conv_b,
                lru_w_a, lru_b_a, lru_w_i, lru_b_i, lru_lambda, w_branch_rnn, q_norm, w_uq, w_qr, kv_norm, w_uk,
                w_uv, w_branch_attn, w_out, mix_post, ffn2_pre, ffn2_w_gate, ffn2_w_up, ffn2_w_down, ffn2_post,
                ple_gate, ple_proj, ple_post):
    vec = lambda a: a[i].reshape(1, -1)
    w_in_i = w_in[i]
    w_kr = w_in_i[:, O_KV:O_KR]
    pad_r = HEAD_PAD - QK_ROPE - QK_NOPE
    w_qr_i, w_uq_i = w_qr[i], w_uq[i]
    zeros_n = jnp.zeros_like(w_uq_i)
    blockdiag = lambda wb: (jnp.eye(RNN_BLOCKS, dtype=F32)[:, None, :, None] * wb[:, :, None, :]).reshape(D_RNN, D_RNN)
    return {
        "ffn1": (vec(ffn1_pre), ffn1_w_gate[i].astype(BF16), ffn1_w_up[i].astype(BF16),
                 ffn1_w_down[i].astype(BF16), vec(ffn1_post)),
        "ffn2": (vec(ffn2_pre), ffn2_w_gate[i].astype(BF16), ffn2_w_up[i].astype(BF16),
                 ffn2_w_down[i].astype(BF16), vec(ffn2_post)),
        "mix_pre": vec(mix_pre),
        "w_main": jnp.concatenate([w_in_i[:, :O_KV], w_in_i[:, O_KR:]], axis=1).astype(BF16),
        "w_kr2": jnp.concatenate([_pad_axis(w_kr, 1, 0, HEAD_PAD - QK_ROPE),
                                  _pad_axis(_rot_cols(w_kr), 1, 0, HEAD_PAD - QK_ROPE)], axis=1).astype(BF16),
        "q_norm": vec(q_norm), "kv_norm": vec(kv_norm),
        "w_qt": _pad_axis(jnp.transpose(jnp.concatenate([w_qr_i, w_uq_i], axis=-1), (1, 2, 0)), 1, 0, pad_r
                          ).reshape(N_HEADS * HEAD_PAD, Q_LORA).astype(BF16),
        "w_qt_rot": _pad_axis(jnp.transpose(jnp.concatenate([_rot_cols(w_qr_i), zeros_n], axis=-1), (1, 2, 0)),
                              1, 0, pad_r).reshape(N_HEADS * HEAD_PAD, Q_LORA).astype(BF16),
        "w_uk_pad": _pad_axis(w_uk[i], 2, QK_ROPE, pad_r).reshape(KV_LORA, N_HEADS * HEAD_PAD).astype(BF16),
        "w_uvt_pad": _pad_axis(jnp.transpose(w_uv[i], (1, 2, 0)), 1, 0, HEAD_PAD - V_HEAD
                               ).reshape(N_HEADS * HEAD_PAD, KV_LORA).astype(BF16),
        "w_uv_pad": _pad_axis(w_uv[i], 2, 0, HEAD_PAD - V_HEAD).reshape(KV_LORA, N_HEADS * HEAD_PAD).astype(BF16),
        "w_uq": w_uq_i.reshape(Q_LORA, N_HEADS * QK_NOPE).astype(BF16),
        "w_qr": w_qr_i.reshape(Q_LORA, N_HEADS * QK_ROPE).astype(BF16),
        "w_qr_rot": _rot_cols(w_qr_i).reshape(Q_LORA, N_HEADS * QK_ROPE).astype(BF16),
        "w_ukt3": jnp.transpose(w_uk[i], (1, 2, 0)).astype(BF16),
        "conv_w": conv_w[i], "conv_b": vec(conv_b),
        "w_lru": jnp.concatenate([blockdiag(lru_w_a[i]), blockdiag(lru_w_i[i])], axis=1).astype(BF16),
        "b_lru": jnp.concatenate([lru_b_a[i].reshape(1, -1), lru_b_i[i].reshape(1, -1)], axis=1),
        "lam": vec(lru_lambda),
        "w_branch_rnn": w_branch_rnn[i].astype(BF16),
        "w_attn_pad": _pad_axis(w_branch_attn[i].reshape(N_HEADS, V_HEAD, D_MODEL), 1, 0, HEAD_PAD - V_HEAD
                                ).reshape(N_HEADS * HEAD_PAD, D_MODEL).astype(BF16),
        "w_out": w_out[i].astype(BF16), "mix_post": vec(mix_post),
        "ple_gate": ple_gate[i].astype(BF16), "ple_proj": ple_proj[i].astype(BF16), "ple_post": vec(ple_post),
    }


def _rope_tables(pos, *, prompt):
    half = QK_ROPE // 2
    freqs = ROPE_THETA ** (-jnp.arange(half, dtype=F32) / half)
    ang = pos.astype(F32)[:, None] * freqs[None, :]
    cos32 = jnp.tile(jnp.cos(ang), (1, 2))
    sin32 = jnp.tile(jnp.sin(ang), (1, 2))
    n = pos.shape[0]
    tabs = {"cos_k": _pad_axis(cos32, 1, 0, HEAD_PAD - QK_ROPE), "sin_k": _pad_axis(sin32, 1, 0, HEAD_PAD - QK_ROPE)}
    if prompt:
        tabs["cos_qt"] = Q_SCALE * jnp.concatenate(
            [cos32.T, jnp.ones((QK_NOPE, n), F32), jnp.zeros((HEAD_PAD - QK_ROPE - QK_NOPE, n), F32)], axis=0)
        tabs["sin_qt"] = Q_SCALE * _pad_axis(sin32.T, 0, 0, HEAD_PAD - QK_ROPE)
    else:
        tabs["cos512"] = Q_SCALE * jnp.tile(cos32, (1, N_HEADS))
        tabs["sin512"] = Q_SCALE * jnp.tile(sin32, (1, N_HEADS))
    return tabs


def _tile(n, pref):
    return pref if n % pref == 0 else n


def _layer_prompt(x, p, w):
    b, s, _ = x.shape
    n = b * s
    tm = _tile(n, 512)
    tabs = _rope_tables(jnp.arange(s, dtype=jnp.int32), prompt=True)
    x1 = _ffn(x.reshape(n, D_MODEL), *w["ffn1"], tm=tm)
    xr, gy, ckv, krope, ga, gb, qt, k, vt = _proj(x1.reshape(b, s, D_MODEL), w, tabs, prompt=True, tm=_tile(s, 256))
    hg, h_last, conv_new = _rglru_prompt(xr, gy, w, ts=_tile(s, 256))
    o = _attn_prompt(qt, k, vt, t=_tile(s, 512))
    x2 = _merge(x1, hg.reshape(n, D_RNN), o.reshape(n, N_HEADS * HEAD_PAD), ga.reshape(n, D_MODEL),
                gb.reshape(n, D_MODEL), w, tm=tm)
    x3 = _ffn(x2, *w["ffn2"], tm=tm)
    y = _ple(x3, p.reshape(n, D_PLE), w, tm=tm)
    return y.reshape(b, s, D_MODEL), (ckv, krope, h_last.reshape(b, D_RNN), conv_new)


def _layer_sample(x, p, h0, conv_buf, pool_ckv, pool_kr, page_table, w):
    n, s, _ = x.shape
    past_len = page_table.shape[1] * PAGE_SIZE
    tabs = _rope_tables(jnp.full((n,), past_len, jnp.int32), prompt=False)
    x1 = _ffn(x.reshape(n, D_MODEL), *w["ffn1"], tm=n)
    xr, gy, ckv, krope, ga, gb, qn, qr = _proj(x1.reshape(1, n, D_MODEL), w, tabs, prompt=False, tm=n)
    hg, h_new, conv_new = _rglru_sample(xr[0], gy[0], jnp.transpose(conv_buf, (1, 0, 2)), h0, w)
    qa = _absorb(qn[0], w["w_ukt3"]).reshape(n, N_HEADS, KV_LORA)
    o_lat = _decode(page_table, qa, qr.reshape(n, N_HEADS, QK_ROPE), ckv.reshape(n, 1, KV_LORA),
                    krope.reshape(n, 1, QK_ROPE), pool_ckv, jnp.transpose(pool_kr, (0, 2, 1)))
    o = _unabsorb(o_lat.reshape(n, N_HEADS * KV_LORA), w["w_uv_pad"])
    x2 = _merge(x1, hg, o, ga[0], gb[0], w, tm=n)
    x3 = _ffn(x2, *w["ffn2"], tm=n)
    y = _ple(x3, p.reshape(n, D_PLE), w, tm=n)
    return (y.reshape(n, s, D_MODEL),
            (ckv.reshape(n, s, KV_LORA), krope.reshape(n, s, QK_ROPE), h_new, jnp.transpose(conv_new, (1, 0, 2))))


def kernel(x_prompt, x_sample, p_prompt, p_sample, cache_ckv, cache_krope, state_h, state_conv, page_table,
           ffn1_pre, ffn1_w_gate, ffn1_w_up, ffn1_w_down, ffn1_post, mix_pre, w_in, conv_w, conv_b, lru_w_a,
           lru_b_a, lru_w_i, lru_b_i, lru_lambda, w_branch_rnn, q_norm, w_uq, w_qr, kv_norm, w_uk, w_uv,
           w_branch_attn, w_out, mix_post, ffn2_pre, ffn2_w_gate, ffn2_w_up, ffn2_w_down, ffn2_post, ple_gate,
           ple_proj, ple_post):
    assert x_sample.shape[1] == 1, "the sample group carries one new token per sequence"
    depth = ffn1_pre.shape[0]
    hp, hs = x_prompt, x_sample
    st_p, st_s = [], []
    for i in range(depth):
        w = _prep_layer(i, ffn1_pre, ffn1_w_gate, ffn1_w_up, ffn1_w_down, ffn1_post, mix_pre, w_in, conv_w, conv_b,
                        lru_w_a, lru_b_a, lru_w_i, lru_b_i, lru_lambda, w_branch_rnn, q_norm, w_uq, w_qr, kv_norm,
                        w_uk, w_uv, w_branch_attn, w_out, mix_post, ffn2_pre, ffn2_w_gate, ffn2_w_up, ffn2_w_down,
                        ffn2_post, ple_gate, ple_proj, ple_post)
        hp, sp = _layer_prompt(hp, p_prompt[i], w)
        hs, ss = _layer_sample(hs, p_sample[i], state_h[i], state_conv[i], cache_ckv[i], cache_krope[i],
                               page_table, w)
        st_p.append(sp)
        st_s.append(ss)
    stack = lambda sts, k: jnp.stack([s[k] for s in sts])
    return (hp, hs, stack(st_p, 0), stack(st_p, 1), stack(st_p, 2), stack(st_p, 3),
            stack(st_s, 0), stack(st_s, 1), stack(st_s, 2), stack(st_s, 3))
```

```python
import functools

import jax
import jax.numpy as jnp
from jax import lax
from jax.experimental import pallas as pl
from jax.experimental.pallas import tpu as pltpu

D_MODEL = 1024
D_RNN = 1280
RNN_BLOCKS = 16
RNN_BLOCK = D_RNN // RNN_BLOCKS
CONV_W = 4
LRU_C = 8.0
N_HEADS = 16
QK_NOPE = 64
QK_ROPE = 32
V_HEAD = 64
Q_LORA = 384
KV_LORA = 256
ROPE_THETA = 10000.0
SM_SCALE = (QK_NOPE + QK_ROPE) ** -0.5
Q_SCALE = SM_SCALE * 1.4426950408889634
D_FF = 2816
D_PLE = 256
EPS = 1e-6
PAGE_SIZE = 128
O_XR = D_RNN
O_YR = O_XR + D_RNN
O_Q = O_YR + Q_LORA
O_KV = O_Q + KV_LORA
O_KR = O_KV + QK_ROPE

LANES = 128
HEAD_PAD = LANES
VMEM_LIMIT_BYTES = 56 * 1024 * 1024

F32 = jnp.float32
BF16 = jnp.bfloat16


def _rms(x, g):
    return x * lax.rsqrt(jnp.mean(x * x, axis=-1, keepdims=True) + EPS) * g


def _dot(a, b):
    return jnp.dot(a, b, preferred_element_type=F32)


def _dot_nt(a, b):
    return lax.dot_general(a, b, (((1,), (1,)), ((), ())), preferred_element_type=F32)


def _resident(shape):
    nd = len(shape)
    return pl.BlockSpec(shape, lambda *_: (0,) * nd, pipeline_mode=pl.Buffered(1))


def _params(*sem):
    return pltpu.CompilerParams(dimension_semantics=sem, vmem_limit_bytes=VMEM_LIMIT_BYTES)


def _ffn_block(x, pre_ref, wg_ref, wu_ref, wd_ref, post_ref, f_chunk):
    u = _rms(x, pre_ref[...]).astype(BF16)
    acc = jnp.zeros(x.shape, F32)
    for c in range(D_FF // f_chunk):
        sl = slice(c * f_chunk, (c + 1) * f_chunk)
        g = _dot(u, wg_ref[:, sl])
        h = (g * jax.nn.sigmoid(g)) * _dot(u, wu_ref[:, sl])
        acc = acc + _dot(h.astype(BF16), wd_ref[sl, :])
    return x + 0.5 * _rms(acc, post_ref[...])


def _ffn_kernel(x_ref, pre_ref, wg_ref, wu_ref, wd_ref, post_ref, o_ref, *, f_chunk):
    o_ref[...] = _ffn_block(x_ref[...], pre_ref, wg_ref, wu_ref, wd_ref, post_ref, f_chunk)


def _ffn(x, pre, wg, wu, wd, post, *, tm):
    n = x.shape[0]
    return pl.pallas_call(
        functools.partial(_ffn_kernel, f_chunk=D_FF // 2),
        grid=(n // tm,),
        in_specs=[pl.BlockSpec((tm, D_MODEL), lambda i: (i, 0)),
                  _resident((1, D_MODEL)), _resident((D_MODEL, D_FF)), _resident((D_MODEL, D_FF)),
                  _resident((D_FF, D_MODEL)), _resident((1, D_MODEL))],
        out_specs=pl.BlockSpec((tm, D_MODEL), lambda i: (i, 0)),
        out_shape=jax.ShapeDtypeStruct((n, D_MODEL), F32),
        compiler_params=_params("parallel"),
        name="ffn",
    )(x, pre, wg, wu, wd, post)


N_MAIN = 2 * D_RNN + Q_LORA + KV_LORA + 2 * D_MODEL
M_YR = D_RNN
M_CQ = 2 * D_RNN
M_KV = M_CQ + Q_LORA
M_GA = M_KV + KV_LORA
M_GB = M_GA + D_MODEL


def _proj_kernel(*refs, prompt):
    (x_ref, pre_ref, wmain_ref, qn_ref, kvn_ref, wkr_ref, cosk_ref, sink_ref), refs = refs[:8], refs[8:]
    if prompt:
        (wqrt_ref, wqrtrot_ref, wuqt_ref, cosqt_ref, sinqt_ref, wuk_ref, wuvt_ref), refs = refs[:7], refs[7:]
    else:
        (wuq_ref, wqr_ref, wqrrot_ref, cos512_ref, sin512_ref), refs = refs[:5], refs[5:]
    xr_ref, gy_ref, ckv_ref, kr_ref, ga_ref, gb_ref = refs[:6]

    u = _rms(x_ref[0], pre_ref[...]).astype(BF16)
    xr_ref[0] = _dot(u, wmain_ref[:, 0:M_YR])
    gy_ref[0] = jax.nn.gelu(_dot(u, wmain_ref[:, M_YR:M_CQ])).astype(BF16)
    ga_ref[0] = jax.nn.sigmoid(_dot(u, wmain_ref[:, M_GA:M_GB])).astype(BF16)
    gb_ref[0] = jax.nn.sigmoid(_dot(u, wmain_ref[:, M_GB:N_MAIN])).astype(BF16)
    cq = _rms(_dot(u, wmain_ref[:, M_CQ:M_KV]), qn_ref[...]).astype(BF16)
    ckv = _rms(_dot(u, wmain_ref[:, M_KV:M_GA]), kvn_ref[...])
    ckv_ref[0] = ckv
    ckv_b = ckv.astype(BF16)
    kr2 = _dot(u, wkr_ref[...])
    kr = kr2[:, :HEAD_PAD] * cosk_ref[...] + kr2[:, HEAD_PAD:] * sink_ref[...]
    kr_ref[0] = kr[:, :QK_ROPE]

    if prompt:
        qt_ref, k_ref, vt_ref = refs[6:]
        tm = cq.shape[0]
        k_nope = _dot(ckv_b, wuk_ref[...])
        qr_t = _dot_nt(wqrt_ref[...], cq)
        qrot_t = _dot_nt(wqrtrot_ref[...], cq)
        qn_t = _dot_nt(wuqt_ref[...], cq) * Q_SCALE
        v_t = _dot_nt(wuvt_ref[...], ckv_b)
        cos_t = cosqt_ref[...]
        sin_t = sinqt_ref[...]
        q_pad = jnp.zeros((HEAD_PAD - QK_ROPE - QK_NOPE, tm), BF16)
        ones_row = (lax.broadcasted_iota(jnp.int32, (HEAD_PAD - V_HEAD, tm), 0) == 0).astype(BF16)
        for h in range(N_HEADS):
            r0 = h * HEAD_PAD
            k_ref[0, h] = (k_nope[:, r0:r0 + HEAD_PAD] + kr).astype(BF16)
            rope = slice(h * QK_ROPE, (h + 1) * QK_ROPE)
            qt_ref[0, r0:r0 + QK_ROPE, :] = (qr_t[rope] * cos_t + qrot_t[rope] * sin_t).astype(BF16)
            qt_ref[0, r0 + QK_ROPE:r0 + QK_ROPE + QK_NOPE, :] = qn_t[h * QK_NOPE:(h + 1) * QK_NOPE].astype(BF16)
            qt_ref[0, r0 + QK_ROPE + QK_NOPE:r0 + HEAD_PAD, :] = q_pad
            vt_ref[0, r0:r0 + V_HEAD, :] = v_t[h * V_HEAD:(h + 1) * V_HEAD].astype(BF16)
            vt_ref[0, r0 + V_HEAD:r0 + HEAD_PAD, :] = ones_row
    else:
        qn_out_ref, qr_out_ref = refs[6:]
        qn_out_ref[0] = (_dot(cq, wuq_ref[...]) * Q_SCALE).astype(BF16)
        qr_out_ref[0] = (_dot(cq, wqr_ref[...]) * cos512_ref[...]
                         + _dot(cq, wqrrot_ref[...]) * sin512_ref[...]).astype(BF16)


def _proj(x, w, tabs, *, prompt, tm):
    b, s, _ = x.shape
    row = lambda d: pl.BlockSpec((1, tm, d), lambda bi, i: (bi, i, 0))
    tab = lambda d: pl.BlockSpec((tm, d), lambda bi, i: (i, 0))
    tab_t = pl.BlockSpec((QK_ROPE, tm), lambda bi, i: (0, i))
    in_specs = [row(D_MODEL), _resident((1, D_MODEL)), _resident((D_MODEL, N_MAIN)), _resident((1, Q_LORA)),
                _resident((1, KV_LORA)), _resident((D_MODEL, 2 * HEAD_PAD)), tab(HEAD_PAD), tab(HEAD_PAD)]
    args = [x, w["mix_pre"], w["w_main"], w["q_norm"], w["kv_norm"], w["w_kr2"], tabs["cos_k"], tabs["sin_k"]]
    out_specs = [row(D_RNN), row(D_RNN), row(KV_LORA), row(QK_ROPE), row(D_MODEL), row(D_MODEL)]
    out_shape = [jax.ShapeDtypeStruct((b, s, D_RNN), F32), jax.ShapeDtypeStruct((b, s, D_RNN), BF16),
                 jax.ShapeDtypeStruct((b, s, KV_LORA), F32), jax.ShapeDtypeStruct((b, s, QK_ROPE), F32),
                 jax.ShapeDtypeStruct((b, s, D_MODEL), BF16), jax.ShapeDtypeStruct((b, s, D_MODEL), BF16)]
    hp = N_HEADS * HEAD_PAD
    if prompt:
        in_specs += [_resident((N_HEADS * QK_ROPE, Q_LORA)), _resident((N_HEADS * QK_ROPE, Q_LORA)),
                     _resident((N_HEADS * QK_NOPE, Q_LORA)), tab_t, tab_t,
                     _resident((KV_LORA, hp)), _resident((N_HEADS * V_HEAD, KV_LORA))]
        args += [w["w_qrt"], w["w_qrt_rot"], w["w_uqt"], tabs["cos_qt"], tabs["sin_qt"], w["w_uk_pad"], w["w_uvt"]]
        lanes_major = pl.BlockSpec((1, hp, tm), lambda bi, i: (bi, 0, i))
        out_specs += [lanes_major, pl.BlockSpec((1, N_HEADS, tm, HEAD_PAD), lambda bi, i: (bi, 0, i, 0)), lanes_major]
        out_shape += [jax.ShapeDtypeStruct((b, hp, s), BF16),
                      jax.ShapeDtypeStruct((b, N_HEADS, s, HEAD_PAD), BF16),
                      jax.ShapeDtypeStruct((b, hp, s), BF16)]
    else:
        nn, nr = N_HEADS * QK_NOPE, N_HEADS * QK_ROPE
        in_specs += [_resident((Q_LORA, nn)), _resident((Q_LORA, nr)), _resident((Q_LORA, nr)), tab(nr), tab(nr)]
        args += [w["w_uq"], w["w_qr"], w["w_qr_rot"], tabs["cos512"], tabs["sin512"]]
        out_specs += [row(nn), row(nr)]
        out_shape += [jax.ShapeDtypeStruct((b, s, nn), BF16), jax.ShapeDtypeStruct((b, s, nr), BF16)]
    return pl.pallas_call(
        functools.partial(_proj_kernel, prompt=prompt),
        grid=(b, s // tm),
        in_specs=in_specs, out_specs=out_specs, out_shape=out_shape,
        compiler_params=_params("parallel", "parallel"),
        name="proj_prompt" if prompt else "proj_sample",
    )(*args)


def _softplus(x):
    return jnp.maximum(x, 0.0) + jnp.log1p(jnp.exp(-jnp.abs(x)))


def _lru_coeffs(xc, wbd_ref, bab_ref, lam_ref):
    z = _dot(xc.astype(BF16), wbd_ref[...]) + bab_ref[...]
    r = jax.nn.sigmoid(z[:, :D_RNN])
    gi = jax.nn.sigmoid(z[:, D_RNN:])
    log_a = -LRU_C * r * _softplus(-lam_ref[...])
    a = jnp.exp(log_a)
    th = jnp.tanh(log_a)
    mult = jnp.sqrt(-2.0 * th / (1.0 - th))
    return a, mult * (gi * xc)


def _rglru_prompt_kernel(xr_ref, gy_ref, cw_ref, cb_ref, wbd_ref, bab_ref, lam_ref,
                         hg_ref, hlast_ref, convnew_ref, xbuf, a_s, b_s, h_s, hcar, *, ts):
    t = pl.program_id(1)
    halo = 8

    @pl.when(t == 0)
    def _():
        xbuf[0:halo, :] = jnp.zeros((halo, D_RNN), F32)
        hcar[...] = jnp.zeros((1, D_RNN), F32)

    x = xr_ref[0]
    xbuf[halo:halo + ts, :] = x
    xc = cb_ref[...] + xbuf[halo - 3:halo - 3 + ts, :] * cw_ref[0:1, :]
    xc = xc + xbuf[halo - 2:halo - 2 + ts, :] * cw_ref[1:2, :]
    xc = xc + xbuf[halo - 1:halo - 1 + ts, :] * cw_ref[2:3, :]
    xc = xc + x * cw_ref[3:4, :]
    xbuf[0:halo, :] = x[ts - halo:, :]

    a, b = _lru_coeffs(xc, wbd_ref, bab_ref, lam_ref)
    a_s[...] = a
    b_s[...] = b

    def step(i, h):
        h = a_s[pl.ds(i, 1), :] * h + b_s[pl.ds(i, 1), :]
        h_s[pl.ds(i, 1), :] = h
        return h

    h = lax.fori_loop(0, ts, step, hcar[...], unroll=8)
    hcar[...] = h
    hg_ref[0] = (h_s[...] * gy_ref[0].astype(F32)).astype(BF16)

    @pl.when(t == pl.num_programs(1) - 1)
    def _():
        hlast_ref[0] = h
        convnew_ref[0] = x[ts - (CONV_W - 1):, :]


def _rglru_prompt(xr, gy, w, *, ts):
    b, s, _ = xr.shape
    row = pl.BlockSpec((1, ts, D_RNN), lambda bi, t: (bi, t, 0))
    return pl.pallas_call(
        functools.partial(_rglru_prompt_kernel, ts=ts),
        grid=(b, s // ts),
        in_specs=[row, row, _resident((CONV_W, D_RNN)), _resident((1, D_RNN)), _resident((D_RNN, 2 * D_RNN)),
                  _resident((1, 2 * D_RNN)), _resident((1, D_RNN))],
        out_specs=[row, pl.BlockSpec((1, 1, D_RNN), lambda bi, t: (bi, 0, 0)),
                   pl.BlockSpec((1, CONV_W - 1, D_RNN), lambda bi, t: (bi, 0, 0))],
        out_shape=[jax.ShapeDtypeStruct((b, s, D_RNN), BF16), jax.ShapeDtypeStruct((b, 1, D_RNN), F32),
                   jax.ShapeDtypeStruct((b, CONV_W - 1, D_RNN), F32)],
        scratch_shapes=[pltpu.VMEM((ts + 8, D_RNN), F32), pltpu.VMEM((ts, D_RNN), F32),
                        pltpu.VMEM((ts, D_RNN), F32), pltpu.VMEM((ts, D_RNN), F32), pltpu.VMEM((1, D_RNN), F32)],
        compiler_params=_params("parallel", "arbitrary"),
        name="rglru_prompt",
    )(xr, gy, w["conv_w"], w["conv_b"], w["w_lru"], w["b_lru"], w["lam"])


def _rglru_sample_kernel(xr_ref, gy_ref, sc_ref, h0_ref, cw_ref, cb_ref, wbd_ref, bab_ref, lam_ref,
                         hg_ref, hnew_ref, convnew_ref):
    x = xr_ref[...]
    xc = cb_ref[...] + sc_ref[0] * cw_ref[0:1, :]
    xc = xc + sc_ref[1] * cw_ref[1:2, :]
    xc = xc + sc_ref[2] * cw_ref[2:3, :]
    xc = xc + x * cw_ref[3:4, :]
    a, b = _lru_coeffs(xc, wbd_ref, bab_ref, lam_ref)
    h = a * h0_ref[...] + b
    hnew_ref[...] = h
    hg_ref[...] = (h * gy_ref[...].astype(F32)).astype(BF16)
    convnew_ref[0] = sc_ref[1]
    convnew_ref[1] = sc_ref[2]
    convnew_ref[2] = x


def _rglru_sample(xr, gy, sc, h0, w):
    n = xr.shape[0]
    return pl.pallas_call(
        _rglru_sample_kernel,
        out_shape=[jax.ShapeDtypeStruct((n, D_RNN), BF16), jax.ShapeDtypeStruct((n, D_RNN), F32),
                   jax.ShapeDtypeStruct((CONV_W - 1, n, D_RNN), F32)],
        compiler_params=pltpu.CompilerParams(vmem_limit_bytes=VMEM_LIMIT_BYTES),
        name="rglru_sample",
    )(xr, gy, sc, h0, w["conv_w"], w["conv_b"], w["w_lru"], w["b_lru"], w["lam"])


HEADS_PER_STEP = 2


def _attn_prompt_kernel(qt_ref, k_ref, vt_ref, o_ref, m_s, acc_s, s_a, s_b, *, t):
    qq = pl.program_id(2)
    heads = range(HEADS_PER_STEP)
    m_s[...] = jnp.full(m_s.shape, -jnp.inf, F32)
    acc_s[...] = jnp.zeros(acc_s.shape, F32)

    def rows(hh):
        return slice(hh * HEAD_PAD, (hh + 1) * HEAD_PAD)

    def scores(stage, j, hh):
        k0 = pl.multiple_of(j * t, t)
        return _dot(k_ref[0, hh, pl.ds(k0, t), :], qt_ref[0, rows(hh), stage * t:(stage + 1) * t])

    def consume(stage, s, j, hh, masked=False):
        k0 = pl.multiple_of(j * t, t)
        if masked:
            kpos = lax.broadcasted_iota(jnp.int32, (t, t), 0)
            qpos = lax.broadcasted_iota(jnp.int32, (t, t), 1)
            s = jnp.where(kpos <= qpos, s, -jnp.inf)
        m_old = m_s[stage, hh]
        m_new = jnp.maximum(m_old, jnp.max(s, axis=0, keepdims=True))
        p = jnp.exp2(s - m_new).astype(BF16)
        acc_s[stage, hh] = (jnp.exp2(m_old - m_new) * acc_s[stage, hh]
                            + _dot(vt_ref[0, rows(hh), pl.ds(k0, t)], p))
        m_s[stage, hh] = m_new

    def step(stage, j, src, dst, nxt):
        for hh in heads:
            dst[hh] = scores(nxt[0], nxt[1], hh)
            consume(stage, src[hh], j, hh)

    def finish(stage):
        o_t = jnp.concatenate(
            [acc_s[stage, hh][:V_HEAD] * (1.0 / acc_s[stage, hh][V_HEAD:V_HEAD + 1]) for hh in heads], axis=0)
        o_ref[0, stage * t:(stage + 1) * t, :] = o_t.T.astype(BF16)

    for hh in heads:
        s_a[hh] = scores(0, 0, hh)

    def pair0(i, c):
        step(0, 2 * i, s_a, s_b, (0, 2 * i + 1))
        step(0, 2 * i + 1, s_b, s_a, (0, 2 * i + 2))
        return c

    lax.fori_loop(0, qq, pair0, 0)
    for hh in heads:
        s_b[hh] = scores(1, 0, hh)
        consume(0, s_a[hh], 2 * qq, hh, masked=True)
    finish(0)

    def pair1(i, c):
        step(1, 2 * i, s_b, s_a, (1, 2 * i + 1))
        step(1, 2 * i + 1, s_a, s_b, (1, 2 * i + 2))
        return c

    lax.fori_loop(0, qq, pair1, 0)
    step(1, 2 * qq, s_b, s_a, (1, 2 * qq + 1))
    for hh in heads:
        consume(1, s_a[hh], 2 * qq + 1, hh, masked=True)
    finish(1)


def _attn_prompt(qt, k, vt, *, t):
    b, h, s, _ = k.shape
    hps = HEADS_PER_STEP
    assert hps * V_HEAD == LANES and s % (2 * t) == 0
    return pl.pallas_call(
        functools.partial(_attn_prompt_kernel, t=t),
        grid=(b, h // hps, s // (2 * t)),
        in_specs=[pl.BlockSpec((1, hps * HEAD_PAD, 2 * t), lambda bi, hi, i: (bi, hi, i)),
                  pl.BlockSpec((1, hps, s, HEAD_PAD), lambda bi, hi, i: (bi, hi, 0, 0)),
                  pl.BlockSpec((1, hps * HEAD_PAD, s), lambda bi, hi, i: (bi, hi, 0))],
        out_specs=pl.BlockSpec((1, 2 * t, hps * V_HEAD), lambda bi, hi, i: (bi, i, hi)),
        out_shape=jax.ShapeDtypeStruct((b, s, h * V_HEAD), BF16),
        scratch_shapes=[pltpu.VMEM((2, hps, 1, t), F32), pltpu.VMEM((2, hps, HEAD_PAD, t), F32),
                        pltpu.VMEM((hps, t, t), F32), pltpu.VMEM((hps, t, t), F32)],
        compiler_params=_params("parallel", "parallel", "arbitrary"),
        name="attn_prompt",
    )(qt, k, vt)


def _absorb_kernel(qn_ref, wukt_ref, qa_ref):
    qn = qn_ref[...]
    for h in range(N_HEADS):
        qa_ref[:, h * KV_LORA:(h + 1) * KV_LORA] = _dot(
            qn[:, h * QK_NOPE:(h + 1) * QK_NOPE], wukt_ref[h]).astype(BF16)


def _absorb(qn, wukt3):
    n = qn.shape[0]
    return pl.pallas_call(
        _absorb_kernel,
        out_shape=jax.ShapeDtypeStruct((n, N_HEADS * KV_LORA), BF16),
        compiler_params=pltpu.CompilerParams(vmem_limit_bytes=VMEM_LIMIT_BYTES),
        name="absorb",
    )(qn, wukt3)


def _decode_kernel(pt_ref, qa_ref, qr_ref, cn_ref, kn_ref, pool_ckv, pool_krt, o_ref,
                   ckv_buf, krt_buf, page_bf, sems, *, n_pages):
    b = pl.program_id(0)
    slot = lax.rem(b, 2)

    def page_copies(page, sl, p):
        return (pltpu.make_async_copy(pool_ckv.at[page], ckv_buf.at[sl, p], sems.at[sl, 0]),
                pltpu.make_async_copy(pool_krt.at[page], krt_buf.at[sl, p], sems.at[sl, 1]))

    def start_row(row, sl):
        def body(p, c):
            for cp in page_copies(pt_ref[row * n_pages + p], sl, p):
                cp.start()
            return c
        lax.fori_loop(0, n_pages, body, 0)

    @pl.when(b == 0)
    def _():
        start_row(0, 0)

    @pl.when(b + 1 < pl.num_programs(0))
    def _():
        start_row(b + 1, 1 - slot)

    for p in range(n_pages):
        for cp in page_copies(0, slot, p):
            cp.wait()

    qa = qa_ref[0]
    qr = qr_ref[0]
    s_pages = [None] * n_pages
    for p in range(n_pages):
        page = ckv_buf[slot, p].astype(BF16)
        page_bf[p] = page
        s_pages[p] = _dot_nt(qa, page) + _dot(qr, krt_buf[slot, p].astype(BF16))
    s = jnp.concatenate(s_pages, axis=1)

    cn = cn_ref[0].astype(BF16).astype(F32)
    kn = kn_ref[0].astype(BF16).astype(F32)
    s_new = (jnp.sum(qa.astype(F32) * cn, axis=-1, keepdims=True)
             + jnp.sum(qr.astype(F32) * kn, axis=-1, keepdims=True))
    m = jnp.maximum(jnp.max(s, axis=-1, keepdims=True), s_new)
    p_past = jnp.exp2(s - m)
    p_new = jnp.exp2(s_new - m)
    l = jnp.sum(p_past, axis=-1, keepdims=True) + p_new
    p_past = p_past.astype(BF16)
    acc = p_new * cn
    for p in range(n_pages):
        acc = acc + _dot(p_past[:, p * PAGE_SIZE:(p + 1) * PAGE_SIZE], page_bf[p])
    o_ref[0] = (acc / l).astype(BF16)


def _decode(page_table, qa, qr, ckv_new, kr_new, pool_ckv, pool_krt):
    n, n_pages = page_table.shape
    per_row = lambda d0, d1: pl.BlockSpec((1, d0, d1), lambda bi, pt: (bi, 0, 0))
    hbm = pl.BlockSpec(memory_space=pl.ANY)
    grid_spec = pltpu.PrefetchScalarGridSpec(
        num_scalar_prefetch=1,
        grid=(n,),
        in_specs=[per_row(N_HEADS, KV_LORA), per_row(N_HEADS, QK_ROPE), per_row(1, KV_LORA), per_row(1, QK_ROPE),
                  hbm, hbm],
        out_specs=per_row(N_HEADS, KV_LORA),
        scratch_shapes=[pltpu.VMEM((2, n_pages, PAGE_SIZE, KV_LORA), F32),
                        pltpu.VMEM((2, n_pages, QK_ROPE, PAGE_SIZE), F32),
                        pltpu.VMEM((n_pages, PAGE_SIZE, KV_LORA), BF16),
                        pltpu.SemaphoreType.DMA((2, 2))],
    )
    return pl.pallas_call(
        functools.partial(_decode_kernel, n_pages=n_pages),
        grid_spec=grid_spec,
        out_shape=jax.ShapeDtypeStruct((n, N_HEADS, KV_LORA), BF16),
        compiler_params=_params("arbitrary"),
        name="decode",
    )(page_table.reshape(-1), qa, qr, ckv_new, kr_new, pool_ckv, pool_krt)


def _unabsorb_kernel(ol_ref, wuv_ref, o_ref):
    for h2 in range(N_HEADS // 2):
        pair = jnp.zeros((ol_ref.shape[0], LANES), F32)
        for h in (2 * h2, 2 * h2 + 1):
            pair = pair + _dot(ol_ref[:, h * KV_LORA:(h + 1) * KV_LORA], wuv_ref[h])
        o_ref[:, h2 * LANES:(h2 + 1) * LANES] = pair.astype(BF16)


def _unabsorb(o_lat, w_uv_pair):
    n = o_lat.shape[0]
    return pl.pallas_call(
        _unabsorb_kernel,
        out_shape=jax.ShapeDtypeStruct((n, N_HEADS * V_HEAD), BF16),
        compiler_params=pltpu.CompilerParams(vmem_limit_bytes=VMEM_LIMIT_BYTES),
        name="unabsorb",
    )(o_lat, w_uv_pair)


def _back_kernel(x_ref, hg_ref, o_ref, ga_ref, gb_ref, p_ref, wrnn_ref, wattn_ref, wout_ref, mpost_ref,
                 fpre_ref, wg_ref, wu_ref, wd_ref, fpost_ref, pg_ref, pp_ref, ppost_ref, y_ref, *, f_chunk):
    y_a = _dot(hg_ref[...], wrnn_ref[...])
    y_b = _dot(o_ref[...], wattn_ref[...])
    m = ga_ref[...].astype(F32) * y_a + gb_ref[...].astype(F32) * y_b
    x = x_ref[...] + _rms(_dot(m.astype(BF16), wout_ref[...]), mpost_ref[...])
    x = _ffn_block(x, fpre_ref, wg_ref, wu_ref, wd_ref, fpost_ref, f_chunk)
    e = jax.nn.sigmoid(_dot(x.astype(BF16), pg_ref[...])) * _dot(p_ref[...].astype(BF16), pp_ref[...])
    y_ref[...] = x + _rms(e, ppost_ref[...])


def _back(x, hg, o, ga, gb, p, w, *, tm):
    n = x.shape[0]
    row = lambda d: pl.BlockSpec((tm, d), lambda i: (i, 0))
    weights = [w["w_branch_rnn"], w["w_branch_attn"], w["w_out"], w["mix_post"], *w["ffn2"],
               w["ple_gate"], w["ple_proj"], w["ple_post"]]
    return pl.pallas_call(
        functools.partial(_back_kernel, f_chunk=D_FF // 2),
        grid=(n // tm,),
        in_specs=[row(D_MODEL), row(D_RNN), row(D_MODEL), row(D_MODEL), row(D_MODEL), row(D_PLE)]
                 + [_resident(a.shape) for a in weights],
        out_specs=row(D_MODEL),
        out_shape=jax.ShapeDtypeStruct((n, D_MODEL), F32),
        compiler_params=_params("parallel"),
        name="back",
    )(x, hg, o, ga, gb, p, *weights)


def _rot_cols(w):
    half = QK_ROPE // 2
    return jnp.concatenate([-w[..., half:], w[..., :half]], axis=-1)


def _pad_axis(a, axis, before, after):
    pads = [(0, 0)] * a.ndim
    pads[axis] = (before, after)
    return jnp.pad(a, pads)


def _prep_layer(i, ffn1_pre, ffn1_w_gate, ffn1_w_up, ffn1_w_down, ffn1_post, mix_pre, w_in, conv_w, conv_b,
                lru_w_a, lru_b_a, lru_w_i, lru_b_i, lru_lambda, w_branch_rnn, q_norm, w_uq, w_qr, kv_norm, w_uk,
                w_uv, w_branch_attn, w_out, mix_post, ffn2_pre, ffn2_w_gate, ffn2_w_up, ffn2_w_down, ffn2_post,
                ple_gate, ple_proj, ple_post):
    vec = lambda a: a[i].reshape(1, -1)
    w_in_i = w_in[i]
    w_kr = w_in_i[:, O_KV:O_KR]
    pad_r = HEAD_PAD - QK_ROPE - QK_NOPE
    w_qr_i, w_uq_i = w_qr[i], w_uq[i]
    blockdiag = lambda wb: (jnp.eye(RNN_BLOCKS, dtype=F32)[:, None, :, None] * wb[:, :, None, :]).reshape(D_RNN, D_RNN)
    return {
        "ffn1": (vec(ffn1_pre), ffn1_w_gate[i].astype(BF16), ffn1_w_up[i].astype(BF16),
                 ffn1_w_down[i].astype(BF16), vec(ffn1_post)),
        "ffn2": (vec(ffn2_pre), ffn2_w_gate[i].astype(BF16), ffn2_w_up[i].astype(BF16),
                 ffn2_w_down[i].astype(BF16), vec(ffn2_post)),
        "mix_pre": vec(mix_pre),
        "w_main": jnp.concatenate([w_in_i[:, :O_KV], w_in_i[:, O_KR:]], axis=1).astype(BF16),
        "w_kr2": jnp.concatenate([_pad_axis(w_kr, 1, 0, HEAD_PAD - QK_ROPE),
                                  _pad_axis(_rot_cols(w_kr), 1, 0, HEAD_PAD - QK_ROPE)], axis=1).astype(BF16),
        "q_norm": vec(q_norm), "kv_norm": vec(kv_norm),
        "w_qrt": jnp.transpose(w_qr_i, (1, 2, 0)).reshape(N_HEADS * QK_ROPE, Q_LORA).astype(BF16),
        "w_qrt_rot": jnp.transpose(_rot_cols(w_qr_i), (1, 2, 0)).reshape(N_HEADS * QK_ROPE, Q_LORA).astype(BF16),
        "w_uqt": jnp.transpose(w_uq_i, (1, 2, 0)).reshape(N_HEADS * QK_NOPE, Q_LORA).astype(BF16),
        "w_uk_pad": _pad_axis(w_uk[i], 2, QK_ROPE, pad_r).reshape(KV_LORA, N_HEADS * HEAD_PAD).astype(BF16),
        "w_uvt": jnp.transpose(w_uv[i], (1, 2, 0)).reshape(N_HEADS * V_HEAD, KV_LORA).astype(BF16),
        "w_uv_pair": jnp.transpose(
            jnp.where((jnp.arange(N_HEADS) % 2 == 1)[None, :, None], _pad_axis(w_uv[i], 2, V_HEAD, 0),
                      _pad_axis(w_uv[i], 2, 0, V_HEAD)), (1, 0, 2)).astype(BF16),
        "w_uq": w_uq_i.reshape(Q_LORA, N_HEADS * QK_NOPE).astype(BF16),
        "w_qr": w_qr_i.reshape(Q_LORA, N_HEADS * QK_ROPE).astype(BF16),
        "w_qr_rot": _rot_cols(w_qr_i).reshape(Q_LORA, N_HEADS * QK_ROPE).astype(BF16),
        "w_ukt3": jnp.transpose(w_uk[i], (1, 2, 0)).astype(BF16),
        "conv_w": conv_w[i], "conv_b": vec(conv_b),
        "w_lru": jnp.concatenate([blockdiag(lru_w_a[i]), blockdiag(lru_w_i[i])], axis=1).astype(BF16),
        "b_lru": jnp.concatenate([lru_b_a[i].reshape(1, -1), lru_b_i[i].reshape(1, -1)], axis=1),
        "lam": vec(lru_lambda),
        "w_branch_rnn": w_branch_rnn[i].astype(BF16),
        "w_branch_attn": w_branch_attn[i].astype(BF16),
        "w_out": w_out[i].astype(BF16), "mix_post": vec(mix_post),
        "ple_gate": ple_gate[i].astype(BF16), "ple_proj": ple_proj[i].astype(BF16), "ple_post": vec(ple_post),
    }


def _rope_tables(pos, *, prompt):
    half = QK_ROPE // 2
    freqs = ROPE_THETA ** (-jnp.arange(half, dtype=F32) / half)
    ang = pos.astype(F32)[:, None] * freqs[None, :]
    cos32 = jnp.tile(jnp.cos(ang), (1, 2))
    sin32 = jnp.tile(jnp.sin(ang), (1, 2))
    n = pos.shape[0]
    tabs = {"cos_k": _pad_axis(cos32, 1, 0, HEAD_PAD - QK_ROPE), "sin_k": _pad_axis(sin32, 1, 0, HEAD_PAD - QK_ROPE)}
    if prompt:
        tabs["cos_qt"] = Q_SCALE * cos32.T
        tabs["sin_qt"] = Q_SCALE * sin32.T
    else:
        tabs["cos512"] = Q_SCALE * jnp.tile(cos32, (1, N_HEADS))
        tabs["sin512"] = Q_SCALE * jnp.tile(sin32, (1, N_HEADS))
    return tabs


def _tile(n, pref):
    return pref if n % pref == 0 else n


def _layer_prompt(x, p, w):
    b, s, _ = x.shape
    n = b * s
    tm = _tile(n, 512)
    tabs = _rope_tables(jnp.arange(s, dtype=jnp.int32), prompt=True)
    x1 = _ffn(x.reshape(n, D_MODEL), *w["ffn1"], tm=tm)
    xr, gy, ckv, krope, ga, gb, qt, k, vt = _proj(x1.reshape(b, s, D_MODEL), w, tabs, prompt=True, tm=_tile(s, 256))
    hg, h_last, conv_new = _rglru_prompt(xr, gy, w, ts=_tile(s, 256))
    o = _attn_prompt(qt, k, vt, t=_tile(s, 512))
    y = _back(x1, hg.reshape(n, D_RNN), o.reshape(n, D_MODEL), ga.reshape(n, D_MODEL), gb.reshape(n, D_MODEL),
              p.reshape(n, D_PLE), w, tm=_tile(n, 256))
    return y.reshape(b, s, D_MODEL), (ckv, krope, h_last.reshape(b, D_RNN), conv_new)


def _layer_sample(x, p, h0, conv_buf, pool_ckv, pool_kr, page_table, w):
    n, s, _ = x.shape
    past_len = page_table.shape[1] * PAGE_SIZE
    tabs = _rope_tables(jnp.full((n,), past_len, jnp.int32), prompt=False)
    x1 = _ffn(x.reshape(n, D_MODEL), *w["ffn1"], tm=n)
    xr, gy, ckv, krope, ga, gb, qn, qr = _proj(x1.reshape(1, n, D_MODEL), w, tabs, prompt=False, tm=n)
    hg, h_new, conv_new = _rglru_sample(xr[0], gy[0], jnp.transpose(conv_buf, (1, 0, 2)), h0, w)
    qa = _absorb(qn[0], w["w_ukt3"]).reshape(n, N_HEADS, KV_LORA)
    o_lat = _decode(page_table, qa, qr.reshape(n, N_HEADS, QK_ROPE), ckv.reshape(n, 1, KV_LORA),
                    krope.reshape(n, 1, QK_ROPE), pool_ckv, jnp.transpose(pool_kr, (0, 2, 1)))
    o = _unabsorb(o_lat.reshape(n, N_HEADS * KV_LORA), w["w_uv_pair"])
    y = _back(x1, hg, o, ga[0], gb[0], p.reshape(n, D_PLE), w, tm=n)
    return (y.reshape(n, s, D_MODEL),
            (ckv.reshape(n, s, KV_LORA), krope.reshape(n, s, QK_ROPE), h_new, jnp.transpose(conv_new, (1, 0, 2))))


def kernel(x_prompt, x_sample, p_prompt, p_sample, cache_ckv, cache_krope, state_h, state_conv, page_table,
           ffn1_pre, ffn1_w_gate, ffn1_w_up, ffn1_w_down, ffn1_post, mix_pre, w_in, conv_w, conv_b, lru_w_a,
           lru_b_a, lru_w_i, lru_b_i, lru_lambda, w_branch_rnn, q_norm, w_uq, w_qr, kv_norm, w_uk, w_uv,
           w_branch_attn, w_out, mix_post, ffn2_pre, ffn2_w_gate, ffn2_w_up, ffn2_w_down, ffn2_post, ple_gate,
           ple_proj, ple_post):
    assert x_sample.shape[1] == 1, "the sample group carries one new token per sequence"
    depth = ffn1_pre.shape[0]
    hp, hs = x_prompt, x_sample
    st_p, st_s = [], []
    for i in range(depth):
        w = _prep_layer(i, ffn1_pre, ffn1_w_gate, ffn1_w_up, ffn1_w_down, ffn1_post, mix_pre, w_in, conv_w, conv_b,
                        lru_w_a, lru_b_a, lru_w_i, lru_b_i, lru_lambda, w_branch_rnn, q_norm, w_uq, w_qr, kv_norm,
                        w_uk, w_uv, w_branch_attn, w_out, mix_post, ffn2_pre, ffn2_w_gate, ffn2_w_up, ffn2_w_down,
                        ffn2_post, ple_gate, ple_proj, ple_post)
        hp, sp = _layer_prompt(hp, p_prompt[i], w)
        hs, ss = _layer_sample(hs, p_sample[i], state_h[i], state_conv[i], cache_ckv[i], cache_krope[i],
                               page_table, w)
        st_p.append(sp)
        st_s.append(ss)
    stack = lambda sts, k: jnp.stack([s[k] for s in sts])
    return (hp, hs, stack(st_p, 0), stack(st_p, 1), stack(st_p, 2), stack(st_p, 3),
            stack(st_s, 0), stack(st_s, 1), stack(st_s, 2), stack(st_s, 3))
```

```python
import functools

import jax
import jax.numpy as jnp
from jax import lax
from jax.experimental import pallas as pl
from jax.experimental.pallas import tpu as pltpu

D_MODEL = 1024
D_RNN = 1280
RNN_BLOCKS = 16
RNN_BLOCK = D_RNN // RNN_BLOCKS
CONV_W = 4
LRU_C = 8.0
N_HEADS = 16
QK_NOPE = 64
QK_ROPE = 32
V_HEAD = 64
Q_LORA = 384
KV_LORA = 256
ROPE_THETA = 10000.0
SM_SCALE = (QK_NOPE + QK_ROPE) ** -0.5
Q_SCALE = SM_SCALE * 1.4426950408889634
D_FF = 2816
D_PLE = 256
EPS = 1e-6
PAGE_SIZE = 128
O_XR = D_RNN
O_YR = O_XR + D_RNN
O_Q = O_YR + Q_LORA
O_KV = O_Q + KV_LORA
O_KR = O_KV + QK_ROPE

LANES = 128
HEAD_PAD = LANES
VMEM_LIMIT_BYTES = 56 * 1024 * 1024

F32 = jnp.float32
BF16 = jnp.bfloat16


def _rms(x, g):
    return x * lax.rsqrt(jnp.mean(x * x, axis=-1, keepdims=True) + EPS) * g


def _dot(a, b):
    return jnp.dot(a, b, preferred_element_type=F32)


def _dot_nt(a, b):
    return lax.dot_general(a, b, (((1,), (1,)), ((), ())), preferred_element_type=F32)


def _resident(shape):
    nd = len(shape)
    return pl.BlockSpec(shape, lambda *_: (0,) * nd, pipeline_mode=pl.Buffered(1))


def _params(*sem):
    return pltpu.CompilerParams(dimension_semantics=sem, vmem_limit_bytes=VMEM_LIMIT_BYTES)


def _ffn_block(x, pre_ref, wg_ref, wu_ref, wd_ref, post_ref, f_chunk):
    u = _rms(x, pre_ref[...]).astype(BF16)
    acc = jnp.zeros(x.shape, F32)
    for c in range(D_FF // f_chunk):
        sl = slice(c * f_chunk, (c + 1) * f_chunk)
        g = _dot(u, wg_ref[:, sl])
        h = (g * jax.nn.sigmoid(g)) * _dot(u, wu_ref[:, sl])
        acc = acc + _dot(h.astype(BF16), wd_ref[sl, :])
    return x + 0.5 * _rms(acc, post_ref[...])


def _ffn_kernel(x_ref, pre_ref, wg_ref, wu_ref, wd_ref, post_ref, o_ref, *, f_chunk):
    o_ref[...] = _ffn_block(x_ref[...], pre_ref, wg_ref, wu_ref, wd_ref, post_ref, f_chunk)


def _ffn(x, pre, wg, wu, wd, post, *, tm):
    n = x.shape[0]
    return pl.pallas_call(
        functools.partial(_ffn_kernel, f_chunk=D_FF // 2),
        grid=(n // tm,),
        in_specs=[pl.BlockSpec((tm, D_MODEL), lambda i: (i, 0)),
                  _resident((1, D_MODEL)), _resident((D_MODEL, D_FF)), _resident((D_MODEL, D_FF)),
                  _resident((D_FF, D_MODEL)), _resident((1, D_MODEL))],
        out_specs=pl.BlockSpec((tm, D_MODEL), lambda i: (i, 0)),
        out_shape=jax.ShapeDtypeStruct((n, D_MODEL), F32),
        compiler_params=_params("parallel"),
        name="ffn",
    )(x, pre, wg, wu, wd, post)


N_MAIN = 2 * D_RNN + Q_LORA + KV_LORA + 2 * D_MODEL
M_YR = D_RNN
M_CQ = 2 * D_RNN
M_KV = M_CQ + Q_LORA
M_GA = M_KV + KV_LORA
M_GB = M_GA + D_MODEL


def _proj_kernel(*refs, prompt):
    (x_ref, pre_ref, wmain_ref, qn_ref, kvn_ref, wkr_ref, cosk_ref, sink_ref), refs = refs[:8], refs[8:]
    if prompt:
        (wqrt_ref, wqrtrot_ref, wuqt_ref, cosqt_ref, sinqt_ref, wuk_ref, wuvt_ref), refs = refs[:7], refs[7:]
    else:
        (wuq_ref, wqr_ref, wqrrot_ref, cos512_ref, sin512_ref), refs = refs[:5], refs[5:]
    xr_ref, gy_ref, ckv_ref, kr_ref, ga_ref, gb_ref = refs[:6]

    u = _rms(x_ref[0], pre_ref[...]).astype(BF16)
    xr_ref[0] = _dot(u, wmain_ref[:, 0:M_YR])
    gy_ref[0] = jax.nn.gelu(_dot(u, wmain_ref[:, M_YR:M_CQ])).astype(BF16)
    ga_ref[0] = jax.nn.sigmoid(_dot(u, wmain_ref[:, M_GA:M_GB])).astype(BF16)
    gb_ref[0] = jax.nn.sigmoid(_dot(u, wmain_ref[:, M_GB:N_MAIN])).astype(BF16)
    cq = _rms(_dot(u, wmain_ref[:, M_CQ:M_KV]), qn_ref[...]).astype(BF16)
    ckv = _rms(_dot(u, wmain_ref[:, M_KV:M_GA]), kvn_ref[...])
    ckv_ref[0] = ckv
    ckv_b = ckv.astype(BF16)
    kr2 = _dot(u, wkr_ref[...])
    kr = kr2[:, :HEAD_PAD] * cosk_ref[...] + kr2[:, HEAD_PAD:] * sink_ref[...]
    kr_ref[0] = kr[:, :QK_ROPE]

    if prompt:
        qt_ref, k_ref, vt_ref = refs[6:]
        tm = cq.shape[0]
        k_nope = _dot(ckv_b, wuk_ref[...])
        qr_t = _dot_nt(wqrt_ref[...], cq)
        qrot_t = _dot_nt(wqrtrot_ref[...], cq)
        qn_t = _dot_nt(wuqt_ref[...], cq) * Q_SCALE
        v_t = _dot_nt(wuvt_ref[...], ckv_b)
        cos_t = cosqt_ref[...]
        sin_t = sinqt_ref[...]
        q_pad = jnp.zeros((HEAD_PAD - QK_ROPE - QK_NOPE, tm), BF16)
        ones_row = (lax.broadcasted_iota(jnp.int32, (HEAD_PAD - V_HEAD, tm), 0) == 0).astype(BF16)
        for h in range(N_HEADS):
            r0 = h * HEAD_PAD
            k_ref[0, h] = (k_nope[:, r0:r0 + HEAD_PAD] + kr).astype(BF16)
            rope = slice(h * QK_ROPE, (h + 1) * QK_ROPE)
            qt_ref[0, r0:r0 + QK_ROPE, :] = (qr_t[rope] * cos_t + qrot_t[rope] * sin_t).astype(BF16)
            qt_ref[0, r0 + QK_ROPE:r0 + QK_ROPE + QK_NOPE, :] = qn_t[h * QK_NOPE:(h + 1) * QK_NOPE].astype(BF16)
            qt_ref[0, r0 + QK_ROPE + QK_NOPE:r0 + HEAD_PAD, :] = q_pad
            vt_ref[0, r0:r0 + V_HEAD, :] = v_t[h * V_HEAD:(h + 1) * V_HEAD].astype(BF16)
            vt_ref[0, r0 + V_HEAD:r0 + HEAD_PAD, :] = ones_row
    else:
        qn_out_ref, qr_out_ref = refs[6:]
        qn_out_ref[0] = (_dot(cq, wuq_ref[...]) * Q_SCALE).astype(BF16)
        qr_out_ref[0] = (_dot(cq, wqr_ref[...]) * cos512_ref[...]
                         + _dot(cq, wqrrot_ref[...]) * sin512_ref[...]).astype(BF16)


def _proj(x, w, tabs, *, prompt, tm):
    b, s, _ = x.shape
    row = lambda d: pl.BlockSpec((1, tm, d), lambda bi, i: (bi, i, 0))
    tab = lambda d: pl.BlockSpec((tm, d), lambda bi, i: (i, 0))
    tab_t = pl.BlockSpec((QK_ROPE, tm), lambda bi, i: (0, i))
    in_specs = [row(D_MODEL), _resident((1, D_MODEL)), _resident((D_MODEL, N_MAIN)), _resident((1, Q_LORA)),
                _resident((1, KV_LORA)), _resident((D_MODEL, 2 * HEAD_PAD)), tab(HEAD_PAD), tab(HEAD_PAD)]
    args = [x, w["mix_pre"], w["w_main"], w["q_norm"], w["kv_norm"], w["w_kr2"], tabs["cos_k"], tabs["sin_k"]]
    out_specs = [row(D_RNN), row(D_RNN), row(KV_LORA), row(QK_ROPE), row(D_MODEL), row(D_MODEL)]
    out_shape = [jax.ShapeDtypeStruct((b, s, D_RNN), F32), jax.ShapeDtypeStruct((b, s, D_RNN), BF16),
                 jax.ShapeDtypeStruct((b, s, KV_LORA), F32), jax.ShapeDtypeStruct((b, s, QK_ROPE), F32),
                 jax.ShapeDtypeStruct((b, s, D_MODEL), BF16), jax.ShapeDtypeStruct((b, s, D_MODEL), BF16)]
    hp = N_HEADS * HEAD_PAD
    if prompt:
        in_specs += [_resident((N_HEADS * QK_ROPE, Q_LORA)), _resident((N_HEADS * QK_ROPE, Q_LORA)),
                     _resident((N_HEADS * QK_NOPE, Q_LORA)), tab_t, tab_t,
                     _resident((KV_LORA, hp)), _resident((N_HEADS * V_HEAD, KV_LORA))]
        args += [w["w_qrt"], w["w_qrt_rot"], w["w_uqt"], tabs["cos_qt"], tabs["sin_qt"], w["w_uk_pad"], w["w_uvt"]]
        lanes_major = pl.BlockSpec((1, hp, tm), lambda bi, i: (bi, 0, i))
        out_specs += [lanes_major, pl.BlockSpec((1, N_HEADS, tm, HEAD_PAD), lambda bi, i: (bi, 0, i, 0)), lanes_major]
        out_shape += [jax.ShapeDtypeStruct((b, hp, s), BF16),
                      jax.ShapeDtypeStruct((b, N_HEADS, s, HEAD_PAD), BF16),
                      jax.ShapeDtypeStruct((b, hp, s), BF16)]
    else:
        nn, nr = N_HEADS * QK_NOPE, N_HEADS * QK_ROPE
        in_specs += [_resident((Q_LORA, nn)), _resident((Q_LORA, nr)), _resident((Q_LORA, nr)), tab(nr), tab(nr)]
        args += [w["w_uq"], w["w_qr"], w["w_qr_rot"], tabs["cos512"], tabs["sin512"]]
        out_specs += [row(nn), row(nr)]
        out_shape += [jax.ShapeDtypeStruct((b, s, nn), BF16), jax.ShapeDtypeStruct((b, s, nr), BF16)]
    return pl.pallas_call(
        functools.partial(_proj_kernel, prompt=prompt),
        grid=(b, s // tm),
        in_specs=in_specs, out_specs=out_specs, out_shape=out_shape,
        compiler_params=_params("parallel", "parallel"),
        name="proj_prompt" if prompt else "proj_sample",
    )(*args)


def _softplus(x):
    return jnp.maximum(x, 0.0) + jnp.log1p(jnp.exp(-jnp.abs(x)))


def _sigmoid(z):
    return 0.5 * jnp.tanh(0.5 * z) + 0.5


def _lru_coeffs(xc, wbd_ref, bab_ref, lam_ref):
    z = _dot(xc.astype(BF16), wbd_ref[...]) + bab_ref[...]
    r = _sigmoid(z[:, :D_RNN])
    gi = _sigmoid(z[:, D_RNN:])
    log_a = -LRU_C * r * _softplus(-lam_ref[...])
    a = jnp.exp(log_a)
    th = jnp.tanh(log_a)
    mult = jnp.sqrt(-2.0 * th / (1.0 - th))
    return a, mult * (gi * xc)


SUBLANES = 8


def _shift_rows(x, tail, k):
    rolled = pltpu.roll(x, k, axis=0)
    head_rows = lax.broadcasted_iota(jnp.int32, tail.shape, 0)
    first = jnp.where(head_rows < k, pltpu.roll(tail, k, axis=0), rolled[:SUBLANES])
    return jnp.concatenate([first, rolled[SUBLANES:]], axis=0)


def _rglru_prompt_kernel(xr_ref, gy_ref, cw_ref, cb_ref, wbd_ref, bab_ref, lam_ref,
                         hg_ref, hlast_ref, convnew_ref, tail_s, a_s, b_s, h_s, hcar, *, ts):
    t = pl.program_id(0)
    nb = xr_ref.shape[0]

    @pl.when(t == 0)
    def _():
        tail_s[...] = jnp.zeros(tail_s.shape, F32)
        hcar[...] = jnp.zeros(hcar.shape, F32)

    for bi in range(nb):
        x = xr_ref[bi]
        tail = tail_s[bi]
        xc = cb_ref[...] + _shift_rows(x, tail, 3) * cw_ref[0:1, :]
        xc = xc + _shift_rows(x, tail, 2) * cw_ref[1:2, :]
        xc = xc + _shift_rows(x, tail, 1) * cw_ref[2:3, :]
        xc = xc + x * cw_ref[3:4, :]
        tail_s[bi] = x[ts - SUBLANES:, :]
        a, b = _lru_coeffs(xc, wbd_ref, bab_ref, lam_ref)
        a_s[bi] = a
        b_s[bi] = b

    def step(i, hs):
        out = []
        for bi in range(nb):
            h = a_s[bi, pl.ds(i, 1), :] * hs[bi] + b_s[bi, pl.ds(i, 1), :]
            h_s[bi, pl.ds(i, 1), :] = h
            out.append(h)
        return tuple(out)

    hs = lax.fori_loop(0, ts, step, tuple(hcar[bi] for bi in range(nb)), unroll=8)
    for bi in range(nb):
        hcar[bi] = hs[bi]
        hg_ref[bi] = (h_s[bi] * gy_ref[bi].astype(F32)).astype(BF16)

    @pl.when(t == pl.num_programs(0) - 1)
    def _():
        for bi in range(nb):
            hlast_ref[bi] = hs[bi]
            convnew_ref[bi] = xr_ref[bi, ts - (CONV_W - 1):, :]


def _rglru_prompt(xr, gy, w, *, ts):
    b, s, _ = xr.shape
    rows = pl.BlockSpec((b, ts, D_RNN), lambda t: (0, t, 0))
    whole = lambda d: pl.BlockSpec((b, d, D_RNN), lambda t: (0, 0, 0))
    return pl.pallas_call(
        functools.partial(_rglru_prompt_kernel, ts=ts),
        grid=(s // ts,),
        in_specs=[rows, rows, _resident((CONV_W, D_RNN)), _resident((1, D_RNN)), _resident((D_RNN, 2 * D_RNN)),
                  _resident((1, 2 * D_RNN)), _resident((1, D_RNN))],
        out_specs=[rows, whole(1), whole(CONV_W - 1)],
        out_shape=[jax.ShapeDtypeStruct((b, s, D_RNN), BF16), jax.ShapeDtypeStruct((b, 1, D_RNN), F32),
                   jax.ShapeDtypeStruct((b, CONV_W - 1, D_RNN), F32)],
        scratch_shapes=[pltpu.VMEM((b, SUBLANES, D_RNN), F32), pltpu.VMEM((b, ts, D_RNN), F32),
                        pltpu.VMEM((b, ts, D_RNN), F32), pltpu.VMEM((b, ts, D_RNN), F32),
                        pltpu.VMEM((b, 1, D_RNN), F32)],
        compiler_params=_params("arbitrary"),
        name="rglru_prompt",
    )(xr, gy, w["conv_w"], w["conv_b"], w["w_lru"], w["b_lru"], w["lam"])


def _rglru_sample_kernel(xr_ref, gy_ref, sc_ref, h0_ref, cw_ref, cb_ref, wbd_ref, bab_ref, lam_ref,
                         hg_ref, hnew_ref, convnew_ref):
    x = xr_ref[...]
    xc = cb_ref[...] + sc_ref[0] * cw_ref[0:1, :]
    xc = xc + sc_ref[1] * cw_ref[1:2, :]
    xc = xc + sc_ref[2] * cw_ref[2:3, :]
    xc = xc + x * cw_ref[3:4, :]
    a, b = _lru_coeffs(xc, wbd_ref, bab_ref, lam_ref)
    h = a * h0_ref[...] + b
    hnew_ref[...] = h
    hg_ref[...] = (h * gy_ref[...].astype(F32)).astype(BF16)
    convnew_ref[0] = sc_ref[1]
    convnew_ref[1] = sc_ref[2]
    convnew_ref[2] = x


def _rglru_sample(xr, gy, sc, h0, w):
    n = xr.shape[0]
    return pl.pallas_call(
        _rglru_sample_kernel,
        out_shape=[jax.ShapeDtypeStruct((n, D_RNN), BF16), jax.ShapeDtypeStruct((n, D_RNN), F32),
                   jax.ShapeDtypeStruct((CONV_W - 1, n, D_RNN), F32)],
        compiler_params=pltpu.CompilerParams(vmem_limit_bytes=VMEM_LIMIT_BYTES),
        name="rglru_sample",
    )(xr, gy, sc, h0, w["conv_w"], w["conv_b"], w["w_lru"], w["b_lru"], w["lam"])


HEADS_PER_STEP = 4


def _attn_prompt_kernel(qt_ref, k_ref, vt_ref, o_ref, m_s, acc_s, s_a, s_b, *, t):
    qq = pl.program_id(2)
    heads = range(HEADS_PER_STEP)
    m_s[...] = jnp.full(m_s.shape, -jnp.inf, F32)
    acc_s[...] = jnp.zeros(acc_s.shape, F32)

    def rows(hh):
        return slice(hh * HEAD_PAD, (hh + 1) * HEAD_PAD)

    def scores(stage, j, hh):
        k0 = pl.multiple_of(j * t, t)
        return _dot(k_ref[0, hh, pl.ds(k0, t), :], qt_ref[0, rows(hh), stage * t:(stage + 1) * t])

    def consume(stage, s, j, hh, masked=False):
        k0 = pl.multiple_of(j * t, t)
        if masked:
            kpos = lax.broadcasted_iota(jnp.int32, (t, t), 0)
            qpos = lax.broadcasted_iota(jnp.int32, (t, t), 1)
            s = jnp.where(kpos <= qpos, s, -jnp.inf)
        m_old = m_s[stage, hh]
        m_new = jnp.maximum(m_old, jnp.max(s, axis=0, keepdims=True))
        p = jnp.exp2(s - m_new).astype(BF16)
        acc_s[stage, hh] = (jnp.exp2(m_old - m_new) * acc_s[stage, hh]
                            + _dot(vt_ref[0, rows(hh), pl.ds(k0, t)], p))
        m_s[stage, hh] = m_new

    def step(stage, j, src, dst, nxt):
        for hh in heads:
            dst[hh] = scores(nxt[0], nxt[1], hh)
            consume(stage, src[hh], j, hh)

    def finish(stage):
        o_t = jnp.concatenate(
            [acc_s[stage, hh][:V_HEAD] * (1.0 / acc_s[stage, hh][V_HEAD:V_HEAD + 1]) for hh in heads], axis=0)
        o_ref[0, stage * t:(stage + 1) * t, :] = o_t.T.astype(BF16)

    for hh in heads:
        s_a[hh] = scores(0, 0, hh)

    def pair0(i, c):
        step(0, 2 * i, s_a, s_b, (0, 2 * i + 1))
        step(0, 2 * i + 1, s_b, s_a, (0, 2 * i + 2))
        return c

    lax.fori_loop(0, qq, pair0, 0)
    for hh in heads:
        s_b[hh] = scores(1, 0, hh)
        consume(0, s_a[hh], 2 * qq, hh, masked=True)
    finish(0)

    def pair1(i, c):
        step(1, 2 * i, s_b, s_a, (1, 2 * i + 1))
        step(1, 2 * i + 1, s_a, s_b, (1, 2 * i + 2))
        return c

    lax.fori_loop(0, qq, pair1, 0)
    step(1, 2 * qq, s_b, s_a, (1, 2 * qq + 1))
    for hh in heads:
        consume(1, s_a[hh], 2 * qq + 1, hh, masked=True)
    finish(1)


def _attn_prompt(qt, k, vt, *, t):
    b, h, s, _ = k.shape
    hps = HEADS_PER_STEP
    assert (hps * V_HEAD) % LANES == 0 and h % hps == 0 and s % (2 * t) == 0
    return pl.pallas_call(
        functools.partial(_attn_prompt_kernel, t=t),
        grid=(b, h // hps, s // (2 * t)),
        in_specs=[pl.BlockSpec((1, hps * HEAD_PAD, 2 * t), lambda bi, hi, i: (bi, hi, i)),
                  pl.BlockSpec((1, hps, s, HEAD_PAD), lambda bi, hi, i: (bi, hi, 0, 0)),
                  pl.BlockSpec((1, hps * HEAD_PAD, s), lambda bi, hi, i: (bi, hi, 0))],
        out_specs=pl.BlockSpec((1, 2 * t, hps * V_HEAD), lambda bi, hi, i: (bi, i, hi)),
        out_shape=jax.ShapeDtypeStruct((b, s, h * V_HEAD), BF16),
        scratch_shapes=[pltpu.VMEM((2, hps, 1, t), F32), pltpu.VMEM((2, hps, HEAD_PAD, t), F32),
                        pltpu.VMEM((hps, t, t), F32), pltpu.VMEM((hps, t, t), F32)],
        compiler_params=_params("parallel", "parallel", "arbitrary"),
        name="attn_prompt",
    )(qt, k, vt)


def _absorb_kernel(qn_ref, wukt_ref, qa_ref):
    qn = qn_ref[...]
    for h in range(N_HEADS):
        qa_ref[:, h * KV_LORA:(h + 1) * KV_LORA] = _dot(
            qn[:, h * QK_NOPE:(h + 1) * QK_NOPE], wukt_ref[h]).astype(BF16)


def _absorb(qn, wukt3):
    n = qn.shape[0]
    return pl.pallas_call(
        _absorb_kernel,
        out_shape=jax.ShapeDtypeStruct((n, N_HEADS * KV_LORA), BF16),
        compiler_params=pltpu.CompilerParams(vmem_limit_bytes=VMEM_LIMIT_BYTES),
        name="absorb",
    )(qn, wukt3)


def _decode_kernel(pt_ref, qa_ref, qr_ref, cn_ref, kn_ref, pool_ckv, pool_krt, o_ref,
                   ckv_buf, krt_buf, page_bf, sems, *, n_pages):
    b = pl.program_id(0)
    slot = lax.rem(b, 2)

    def page_copies(page, sl, p):
        return (pltpu.make_async_copy(pool_ckv.at[page], ckv_buf.at[sl, p], sems.at[sl, 0]),
                pltpu.make_async_copy(pool_krt.at[page], krt_buf.at[sl, p], sems.at[sl, 1]))

    def start_row(row, sl):
        def body(p, c):
            for cp in page_copies(pt_ref[row * n_pages + p], sl, p):
                cp.start()
            return c
        lax.fori_loop(0, n_pages, body, 0)

    @pl.when(b == 0)
    def _():
        start_row(0, 0)

    @pl.when(b + 1 < pl.num_programs(0))
    def _():
        start_row(b + 1, 1 - slot)

    for p in range(n_pages):
        for cp in page_copies(0, slot, p):
            cp.wait()

    qa = qa_ref[0]
    qr = qr_ref[0]
    s_pages = [None] * n_pages
    for p in range(n_pages):
        page = ckv_buf[slot, p].astype(BF16)
        page_bf[p] = page
        s_pages[p] = _dot_nt(qa, page) + _dot(qr, krt_buf[slot, p].astype(BF16))
    s = jnp.concatenate(s_pages, axis=1)

    cn = cn_ref[0].astype(BF16).astype(F32)
    kn = kn_ref[0].astype(BF16).astype(F32)
    s_new = (jnp.sum(qa.astype(F32) * cn, axis=-1, keepdims=True)
             + jnp.sum(qr.astype(F32) * kn, axis=-1, keepdims=True))
    m = jnp.maximum(jnp.max(s, axis=-1, keepdims=True), s_new)
    p_past = jnp.exp2(s - m)
    p_new = jnp.exp2(s_new - m)
    l = jnp.sum(p_past, axis=-1, keepdims=True) + p_new
    p_past = p_past.astype(BF16)
    acc = p_new * cn
    for p in range(n_pages):
        acc = acc + _dot(p_past[:, p * PAGE_SIZE:(p + 1) * PAGE_SIZE], page_bf[p])
    o_ref[0] = (acc / l).astype(BF16)


def _decode(page_table, qa, qr, ckv_new, kr_new, pool_ckv, pool_krt):
    n, n_pages = page_table.shape
    per_row = lambda d0, d1: pl.BlockSpec((1, d0, d1), lambda bi, pt: (bi, 0, 0))
    hbm = pl.BlockSpec(memory_space=pl.ANY)
    grid_spec = pltpu.PrefetchScalarGridSpec(
        num_scalar_prefetch=1,
        grid=(n,),
        in_specs=[per_row(N_HEADS, KV_LORA), per_row(N_HEADS, QK_ROPE), per_row(1, KV_LORA), per_row(1, QK_ROPE),
                  hbm, hbm],
        out_specs=per_row(N_HEADS, KV_LORA),
        scratch_shapes=[pltpu.VMEM((2, n_pages, PAGE_SIZE, KV_LORA), F32),
                        pltpu.VMEM((2, n_pages, QK_ROPE, PAGE_SIZE), F32),
                        pltpu.VMEM((n_pages, PAGE_SIZE, KV_LORA), BF16),
                        pltpu.SemaphoreType.DMA((2, 2))],
    )
    return pl.pallas_call(
        functools.partial(_decode_kernel, n_pages=n_pages),
        grid_spec=grid_spec,
        out_shape=jax.ShapeDtypeStruct((n, N_HEADS, KV_LORA), BF16),
        compiler_params=_params("arbitrary"),
        name="decode",
    )(page_table.reshape(-1), qa, qr, ckv_new, kr_new, pool_ckv, pool_krt)


def _unabsorb_kernel(ol_ref, wuv_ref, o_ref):
    for h2 in range(N_HEADS // 2):
        pair = jnp.zeros((ol_ref.shape[0], LANES), F32)
        for h in (2 * h2, 2 * h2 + 1):
            pair = pair + _dot(ol_ref[:, h * KV_LORA:(h + 1) * KV_LORA], wuv_ref[h])
        o_ref[:, h2 * LANES:(h2 + 1) * LANES] = pair.astype(BF16)


def _unabsorb(o_lat, w_uv_pair):
    n = o_lat.shape[0]
    return pl.pallas_call(
        _unabsorb_kernel,
        out_shape=jax.ShapeDtypeStruct((n, N_HEADS * V_HEAD), BF16),
        compiler_params=pltpu.CompilerParams(vmem_limit_bytes=VMEM_LIMIT_BYTES),
        name="unabsorb",
    )(o_lat, w_uv_pair)


def _back_kernel(x_ref, hg_ref, o_ref, ga_ref, gb_ref, p_ref, wrnn_ref, wattn_ref, wout_ref, mpost_ref,
                 fpre_ref, wg_ref, wu_ref, wd_ref, fpost_ref, pg_ref, pp_ref, ppost_ref, y_ref, *, f_chunk):
    y_a = _dot(hg_ref[...], wrnn_ref[...])
    y_b = _dot(o_ref[...], wattn_ref[...])
    m = ga_ref[...].astype(F32) * y_a + gb_ref[...].astype(F32) * y_b
    x = x_ref[...] + _rms(_dot(m.astype(BF16), wout_ref[...]), mpost_ref[...])
    x = _ffn_block(x, fpre_ref, wg_ref, wu_ref, wd_ref, fpost_ref, f_chunk)
    e = jax.nn.sigmoid(_dot(x.astype(BF16), pg_ref[...])) * _dot(p_ref[...].astype(BF16), pp_ref[...])
    y_ref[...] = x + _rms(e, ppost_ref[...])


def _back(x, hg, o, ga, gb, p, w, *, tm):
    n = x.shape[0]
    row = lambda d: pl.BlockSpec((tm, d), lambda i: (i, 0))
    weights = [w["w_branch_rnn"], w["w_branch_attn"], w["w_out"], w["mix_post"], *w["ffn2"],
               w["ple_gate"], w["ple_proj"], w["ple_post"]]
    return pl.pallas_call(
        functools.partial(_back_kernel, f_chunk=D_FF // 2),
        grid=(n // tm,),
        in_specs=[row(D_MODEL), row(D_RNN), row(D_MODEL), row(D_MODEL), row(D_MODEL), row(D_PLE)]
                 + [_resident(a.shape) for a in weights],
        out_specs=row(D_MODEL),
        out_shape=jax.ShapeDtypeStruct((n, D_MODEL), F32),
        compiler_params=_params("parallel"),
        name="back",
    )(x, hg, o, ga, gb, p, *weights)


def _rot_cols(w):
    half = QK_ROPE // 2
    return jnp.concatenate([-w[..., half:], w[..., :half]], axis=-1)


def _pad_axis(a, axis, before, after):
    pads = [(0, 0)] * a.ndim
    pads[axis] = (before, after)
    return jnp.pad(a, pads)


def _prep_layer(i, ffn1_pre, ffn1_w_gate, ffn1_w_up, ffn1_w_down, ffn1_post, mix_pre, w_in, conv_w, conv_b,
                lru_w_a, lru_b_a, lru_w_i, lru_b_i, lru_lambda, w_branch_rnn, q_norm, w_uq, w_qr, kv_norm, w_uk,
                w_uv, w_branch_attn, w_out, mix_post, ffn2_pre, ffn2_w_gate, ffn2_w_up, ffn2_w_down, ffn2_post,
                ple_gate, ple_proj, ple_post):
    vec = lambda a: a[i].reshape(1, -1)
    w_in_i = w_in[i]
    w_kr = w_in_i[:, O_KV:O_KR]
    pad_r = HEAD_PAD - QK_ROPE - QK_NOPE
    w_qr_i, w_uq_i = w_qr[i], w_uq[i]
    blockdiag = lambda wb: (jnp.eye(RNN_BLOCKS, dtype=F32)[:, None, :, None] * wb[:, :, None, :]).reshape(D_RNN, D_RNN)
    return {
        "ffn1": (vec(ffn1_pre), ffn1_w_gate[i].astype(BF16), ffn1_w_up[i].astype(BF16),
                 ffn1_w_down[i].astype(BF16), vec(ffn1_post)),
        "ffn2": (vec(ffn2_pre), ffn2_w_gate[i].astype(BF16), ffn2_w_up[i].astype(BF16),
                 ffn2_w_down[i].astype(BF16), vec(ffn2_post)),
        "mix_pre": vec(mix_pre),
        "w_main": jnp.concatenate([w_in_i[:, :O_KV], w_in_i[:, O_KR:]], axis=1).astype(BF16),
        "w_kr2": jnp.concatenate([_pad_axis(w_kr, 1, 0, HEAD_PAD - QK_ROPE),
                                  _pad_axis(_rot_cols(w_kr), 1, 0, HEAD_PAD - QK_ROPE)], axis=1).astype(BF16),
        "q_norm": vec(q_norm), "kv_norm": vec(kv_norm),
        "w_qrt": jnp.transpose(w_qr_i, (1, 2, 0)).reshape(N_HEADS * QK_ROPE, Q_LORA).astype(BF16),
        "w_qrt_rot": jnp.transpose(_rot_cols(w_qr_i), (1, 2, 0)).reshape(N_HEADS * QK_ROPE, Q_LORA).astype(BF16),
        "w_uqt": jnp.transpose(w_uq_i, (1, 2, 0)).reshape(N_HEADS * QK_NOPE, Q_LORA).astype(BF16),
        "w_uk_pad": _pad_axis(w_uk[i], 2, QK_ROPE, pad_r).reshape(KV_LORA, N_HEADS * HEAD_PAD).astype(BF16),
        "w_uvt": jnp.transpose(w_uv[i], (1, 2, 0)).reshape(N_HEADS * V_HEAD, KV_LORA).astype(BF16),
        "w_uv_pair": jnp.transpose(
            jnp.where((jnp.arange(N_HEADS) % 2 == 1)[None, :, None], _pad_axis(w_uv[i], 2, V_HEAD, 0),
                      _pad_axis(w_uv[i], 2, 0, V_HEAD)), (1, 0, 2)).astype(BF16),
        "w_uq": w_uq_i.reshape(Q_LORA, N_HEADS * QK_NOPE).astype(BF16),
        "w_qr": w_qr_i.reshape(Q_LORA, N_HEADS * QK_ROPE).astype(BF16),
        "w_qr_rot": _rot_cols(w_qr_i).reshape(Q_LORA, N_HEADS * QK_ROPE).astype(BF16),
        "w_ukt3": jnp.transpose(w_uk[i], (1, 2, 0)).astype(BF16),
        "conv_w": conv_w[i], "conv_b": vec(conv_b),
        "w_lru": jnp.concatenate([blockdiag(lru_w_a[i]), blockdiag(lru_w_i[i])], axis=1).astype(BF16),
        "b_lru": jnp.concatenate([lru_b_a[i].reshape(1, -1), lru_b_i[i].reshape(1, -1)], axis=1),
        "lam": vec(lru_lambda),
        "w_branch_rnn": w_branch_rnn[i].astype(BF16),
        "w_branch_attn": w_branch_attn[i].astype(BF16),
        "w_out": w_out[i].astype(BF16), "mix_post": vec(mix_post),
        "ple_gate": ple_gate[i].astype(BF16), "ple_proj": ple_proj[i].astype(BF16), "ple_post": vec(ple_post),
    }


def _rope_tables(pos, *, prompt):
    half = QK_ROPE // 2
    freqs = ROPE_THETA ** (-jnp.arange(half, dtype=F32) / half)
    ang = pos.astype(F32)[:, None] * freqs[None, :]
    cos32 = jnp.tile(jnp.cos(ang), (1, 2))
    sin32 = jnp.tile(jnp.sin(ang), (1, 2))
    n = pos.shape[0]
    tabs = {"cos_k": _pad_axis(cos32, 1, 0, HEAD_PAD - QK_ROPE), "sin_k": _pad_axis(sin32, 1, 0, HEAD_PAD - QK_ROPE)}
    if prompt:
        tabs["cos_qt"] = Q_SCALE * cos32.T
        tabs["sin_qt"] = Q_SCALE * sin32.T
    else:
        tabs["cos512"] = Q_SCALE * jnp.tile(cos32, (1, N_HEADS))
        tabs["sin512"] = Q_SCALE * jnp.tile(sin32, (1, N_HEADS))
    return tabs


def _tile(n, pref):
    return pref if n % pref == 0 else n


def _layer_prompt(x, p, w):
    b, s, _ = x.shape
    n = b * s
    tm = _tile(n, 512)
    tabs = _rope_tables(jnp.arange(s, dtype=jnp.int32), prompt=True)
    x1 = _ffn(x.reshape(n, D_MODEL), *w["ffn1"], tm=tm)
    xr, gy, ckv, krope, ga, gb, qt, k, vt = _proj(x1.reshape(b, s, D_MODEL), w, tabs, prompt=True, tm=_tile(s, 256))
    hg, h_last, conv_new = _rglru_prompt(xr, gy, w, ts=_tile(s, 256))
    o = _attn_prompt(qt, k, vt, t=_tile(s, 512))
    y = _back(x1, hg.reshape(n, D_RNN), o.reshape(n, D_MODEL), ga.reshape(n, D_MODEL), gb.reshape(n, D_MODEL),
              p.reshape(n, D_PLE), w, tm=_tile(n, 256))
    return y.reshape(b, s, D_MODEL), (ckv, krope, h_last.reshape(b, D_RNN), conv_new)


def _layer_sample(x, p, h0, conv_buf, pool_ckv, pool_kr, page_table, w):
    n, s, _ = x.shape
    past_len = page_table.shape[1] * PAGE_SIZE
    tabs = _rope_tables(jnp.full((n,), past_len, jnp.int32), prompt=False)
    x1 = _ffn(x.reshape(n, D_MODEL), *w["ffn1"], tm=n)
    xr, gy, ckv, krope, ga, gb, qn, qr = _proj(x1.reshape(1, n, D_MODEL), w, tabs, prompt=False, tm=n)
    hg, h_new, conv_new = _rglru_sample(xr[0], gy[0], jnp.transpose(conv_buf, (1, 0, 2)), h0, w)
    qa = _absorb(qn[0], w["w_ukt3"]).reshape(n, N_HEADS, KV_LORA)
    o_lat = _decode(page_table, qa, qr.reshape(n, N_HEADS, QK_ROPE), ckv.reshape(n, 1, KV_LORA),
                    krope.reshape(n, 1, QK_ROPE), pool_ckv, jnp.transpose(pool_kr, (0, 2, 1)))
    o = _unabsorb(o_lat.reshape(n, N_HEADS * KV_LORA), w["w_uv_pair"])
    y = _back(x1, hg, o, ga[0], gb[0], p.reshape(n, D_PLE), w, tm=n)
    return (y.reshape(n, s, D_MODEL),
            (ckv.reshape(n, s, KV_LORA), krope.reshape(n, s, QK_ROPE), h_new, jnp.transpose(conv_new, (1, 0, 2))))


def kernel(x_prompt, x_sample, p_prompt, p_sample, cache_ckv, cache_krope, state_h, state_conv, page_table,
           ffn1_pre, ffn1_w_gate, ffn1_w_up, ffn1_w_down, ffn1_post, mix_pre, w_in, conv_w, conv_b, lru_w_a,
           lru_b_a, lru_w_i, lru_b_i, lru_lambda, w_branch_rnn, q_norm, w_uq, w_qr, kv_norm, w_uk, w_uv,
           w_branch_attn, w_out, mix_post, ffn2_pre, ffn2_w_gate, ffn2_w_up, ffn2_w_down, ffn2_post, ple_gate,
           ple_proj, ple_post):
    assert x_sample.shape[1] == 1, "the sample group carries one new token per sequence"
    depth = ffn1_pre.shape[0]
    hp, hs = x_prompt, x_sample
    st_p, st_s = [], []
    for i in range(depth):
        w = _prep_layer(i, ffn1_pre, ffn1_w_gate, ffn1_w_up, ffn1_w_down, ffn1_post, mix_pre, w_in, conv_w, conv_b,
                        lru_w_a, lru_b_a, lru_w_i, lru_b_i, lru_lambda, w_branch_rnn, q_norm, w_uq, w_qr, kv_norm,
                        w_uk, w_uv, w_branch_attn, w_out, mix_post, ffn2_pre, ffn2_w_gate, ffn2_w_up, ffn2_w_down,
                        ffn2_post, ple_gate, ple_proj, ple_post)
        hp, sp = _layer_prompt(hp, p_prompt[i], w)
        hs, ss = _layer_sample(hs, p_sample[i], state_h[i], state_conv[i], cache_ckv[i], cache_krope[i],
                               page_table, w)
        st_p.append(sp)
        st_s.append(ss)
    stack = lambda sts, k: jnp.stack([s[k] for s in sts])
    return (hp, hs, stack(st_p, 0), stack(st_p, 1), stack(st_p, 2), stack(st_p, 3),
            stack(st_s, 0), stack(st_s, 1), stack(st_s, 2), stack(st_s, 3))
```

```python
import functools

import jax
import jax.numpy as jnp
from jax import lax
from jax.experimental import pallas as pl
from jax.experimental.pallas import tpu as pltpu

D_MODEL = 1024
D_RNN = 1280
RNN_BLOCKS = 16
RNN_BLOCK = D_RNN // RNN_BLOCKS
CONV_W = 4
LRU_C = 8.0
N_HEADS = 16
QK_NOPE = 64
QK_ROPE = 32
V_HEAD = 64
Q_LORA = 384
KV_LORA = 256
ROPE_THETA = 10000.0
SM_SCALE = (QK_NOPE + QK_ROPE) ** -0.5
Q_SCALE = SM_SCALE * 1.4426950408889634
D_FF = 2816
D_PLE = 256
EPS = 1e-6
PAGE_SIZE = 128
O_XR = D_RNN
O_YR = O_XR + D_RNN
O_Q = O_YR + Q_LORA
O_KV = O_Q + KV_LORA
O_KR = O_KV + QK_ROPE

LANES = 128
HEAD_PAD = LANES
VMEM_LIMIT_BYTES = 56 * 1024 * 1024

F32 = jnp.float32
BF16 = jnp.bfloat16


def _rms(x, g):
    return x * lax.rsqrt(jnp.mean(x * x, axis=-1, keepdims=True) + EPS) * g


def _dot(a, b):
    return jnp.dot(a, b, preferred_element_type=F32)


def _dot_nt(a, b):
    return lax.dot_general(a, b, (((1,), (1,)), ((), ())), preferred_element_type=F32)


def _resident(shape):
    nd = len(shape)
    return pl.BlockSpec(shape, lambda *_: (0,) * nd, pipeline_mode=pl.Buffered(1))


def _params(*sem):
    return pltpu.CompilerParams(dimension_semantics=sem, vmem_limit_bytes=VMEM_LIMIT_BYTES)


def _ffn_block(x, pre_ref, wg_ref, wu_ref, wd_ref, post_ref, f_chunk):
    u = _rms(x, pre_ref[...]).astype(BF16)
    acc = jnp.zeros(x.shape, F32)
    for c in range(D_FF // f_chunk):
        sl = slice(c * f_chunk, (c + 1) * f_chunk)
        g = _dot(u, wg_ref[:, sl])
        h = (g * jax.nn.sigmoid(g)) * _dot(u, wu_ref[:, sl])
        acc = acc + _dot(h.astype(BF16), wd_ref[sl, :])
    return x + 0.5 * _rms(acc, post_ref[...])


def _ffn_kernel(x_ref, pre_ref, wg_ref, wu_ref, wd_ref, post_ref, o_ref, *, f_chunk):
    o_ref[...] = _ffn_block(x_ref[...], pre_ref, wg_ref, wu_ref, wd_ref, post_ref, f_chunk)


def _ffn(x, pre, wg, wu, wd, post, *, tm):
    n = x.shape[0]
    return pl.pallas_call(
        functools.partial(_ffn_kernel, f_chunk=D_FF // 2),
        grid=(n // tm,),
        in_specs=[pl.BlockSpec((tm, D_MODEL), lambda i: (i, 0)),
                  _resident((1, D_MODEL)), _resident((D_MODEL, D_FF)), _resident((D_MODEL, D_FF)),
                  _resident((D_FF, D_MODEL)), _resident((1, D_MODEL))],
        out_specs=pl.BlockSpec((tm, D_MODEL), lambda i: (i, 0)),
        out_shape=jax.ShapeDtypeStruct((n, D_MODEL), F32),
        compiler_params=_params("parallel"),
        name="ffn",
    )(x, pre, wg, wu, wd, post)


N_MAIN = 2 * D_RNN + Q_LORA + KV_LORA + 2 * D_MODEL
M_YR = D_RNN
M_CQ = 2 * D_RNN
M_KV = M_CQ + Q_LORA
M_GA = M_KV + KV_LORA
M_GB = M_GA + D_MODEL


def _proj_kernel(*refs, prompt):
    (x_ref, pre_ref, wmain_ref, qn_ref, kvn_ref, wkr_ref, cosk_ref, sink_ref), refs = refs[:8], refs[8:]
    if prompt:
        (wqrt_ref, wqrtrot_ref, wuqt_ref, cosqt_ref, sinqt_ref, wuk_ref, wuvt_ref), refs = refs[:7], refs[7:]
    else:
        (wuq_ref, wqr_ref, wqrrot_ref, cos512_ref, sin512_ref), refs = refs[:5], refs[5:]
    xr_ref, gy_ref, ckv_ref, kr_ref, ga_ref, gb_ref = refs[:6]

    u = _rms(x_ref[0], pre_ref[...]).astype(BF16)
    xr_ref[0] = _dot(u, wmain_ref[:, 0:M_YR])
    gy_ref[0] = jax.nn.gelu(_dot(u, wmain_ref[:, M_YR:M_CQ])).astype(BF16)
    ga_ref[0] = jax.nn.sigmoid(_dot(u, wmain_ref[:, M_GA:M_GB])).astype(BF16)
    gb_ref[0] = jax.nn.sigmoid(_dot(u, wmain_ref[:, M_GB:N_MAIN])).astype(BF16)
    cq = _rms(_dot(u, wmain_ref[:, M_CQ:M_KV]), qn_ref[...]).astype(BF16)
    ckv = _rms(_dot(u, wmain_ref[:, M_KV:M_GA]), kvn_ref[...])
    ckv_ref[0] = ckv
    ckv_b = ckv.astype(BF16)
    kr2 = _dot(u, wkr_ref[...])
    kr = kr2[:, :HEAD_PAD] * cosk_ref[...] + kr2[:, HEAD_PAD:] * sink_ref[...]
    kr_ref[0] = kr[:, :QK_ROPE]

    if prompt:
        qt_ref, k_ref, vt_ref = refs[6:]
        tm = cq.shape[0]
        k_nope = _dot(ckv_b, wuk_ref[...])
        qr_t = _dot_nt(wqrt_ref[...], cq)
        qrot_t = _dot_nt(wqrtrot_ref[...], cq)
        qn_t = _dot_nt(wuqt_ref[...], cq) * Q_SCALE
        v_t = _dot_nt(wuvt_ref[...], ckv_b)
        cos_t = cosqt_ref[...]
        sin_t = sinqt_ref[...]
        q_pad = jnp.zeros((HEAD_PAD - QK_ROPE - QK_NOPE, tm), BF16)
        ones_row = (lax.broadcasted_iota(jnp.int32, (HEAD_PAD - V_HEAD, tm), 0) == 0).astype(BF16)
        for h in range(N_HEADS):
            r0 = h * HEAD_PAD
            k_ref[0, h] = (k_nope[:, r0:r0 + HEAD_PAD] + kr).astype(BF16)
            rope = slice(h * QK_ROPE, (h + 1) * QK_ROPE)
            qt_ref[0, r0:r0 + QK_ROPE, :] = (qr_t[rope] * cos_t + qrot_t[rope] * sin_t).astype(BF16)
            qt_ref[0, r0 + QK_ROPE:r0 + QK_ROPE + QK_NOPE, :] = qn_t[h * QK_NOPE:(h + 1) * QK_NOPE].astype(BF16)
            qt_ref[0, r0 + QK_ROPE + QK_NOPE:r0 + HEAD_PAD, :] = q_pad
            vt_ref[0, r0:r0 + V_HEAD, :] = v_t[h * V_HEAD:(h + 1) * V_HEAD].astype(BF16)
            vt_ref[0, r0 + V_HEAD:r0 + HEAD_PAD, :] = ones_row
    else:
        qn_out_ref, qr_out_ref = refs[6:]
        qn_out_ref[0] = (_dot(cq, wuq_ref[...]) * Q_SCALE).astype(BF16)
        qr_out_ref[0] = (_dot(cq, wqr_ref[...]) * cos512_ref[...]
                         + _dot(cq, wqrrot_ref[...]) * sin512_ref[...]).astype(BF16)


def _proj(x, w, tabs, *, prompt, tm):
    b, s, _ = x.shape
    row = lambda d: pl.BlockSpec((1, tm, d), lambda bi, i: (bi, i, 0))
    tab = lambda d: pl.BlockSpec((tm, d), lambda bi, i: (i, 0))
    tab_t = pl.BlockSpec((QK_ROPE, tm), lambda bi, i: (0, i))
    in_specs = [row(D_MODEL), _resident((1, D_MODEL)), _resident((D_MODEL, N_MAIN)), _resident((1, Q_LORA)),
                _resident((1, KV_LORA)), _resident((D_MODEL, 2 * HEAD_PAD)), tab(HEAD_PAD), tab(HEAD_PAD)]
    args = [x, w["mix_pre"], w["w_main"], w["q_norm"], w["kv_norm"], w["w_kr2"], tabs["cos_k"], tabs["sin_k"]]
    out_specs = [row(D_RNN), row(D_RNN), row(KV_LORA), row(QK_ROPE), row(D_MODEL), row(D_MODEL)]
    out_shape = [jax.ShapeDtypeStruct((b, s, D_RNN), F32), jax.ShapeDtypeStruct((b, s, D_RNN), BF16),
                 jax.ShapeDtypeStruct((b, s, KV_LORA), F32), jax.ShapeDtypeStruct((b, s, QK_ROPE), F32),
                 jax.ShapeDtypeStruct((b, s, D_MODEL), BF16), jax.ShapeDtypeStruct((b, s, D_MODEL), BF16)]
    hp = N_HEADS * HEAD_PAD
    if prompt:
        in_specs += [_resident((N_HEADS * QK_ROPE, Q_LORA)), _resident((N_HEADS * QK_ROPE, Q_LORA)),
                     _resident((N_HEADS * QK_NOPE, Q_LORA)), tab_t, tab_t,
                     _resident((KV_LORA, hp)), _resident((N_HEADS * V_HEAD, KV_LORA))]
        args += [w["w_qrt"], w["w_qrt_rot"], w["w_uqt"], tabs["cos_qt"], tabs["sin_qt"], w["w_uk_pad"], w["w_uvt"]]
        lanes_major = pl.BlockSpec((1, hp, tm), lambda bi, i: (bi, 0, i))
        out_specs += [lanes_major, pl.BlockSpec((1, N_HEADS, tm, HEAD_PAD), lambda bi, i: (bi, 0, i, 0)), lanes_major]
        out_shape += [jax.ShapeDtypeStruct((b, hp, s), BF16),
                      jax.ShapeDtypeStruct((b, N_HEADS, s, HEAD_PAD), BF16),
                      jax.ShapeDtypeStruct((b, hp, s), BF16)]
    else:
        nn, nr = N_HEADS * QK_NOPE, N_HEADS * QK_ROPE
        in_specs += [_resident((Q_LORA, nn)), _resident((Q_LORA, nr)), _resident((Q_LORA, nr)), tab(nr), tab(nr)]
        args += [w["w_uq"], w["w_qr"], w["w_qr_rot"], tabs["cos512"], tabs["sin512"]]
        out_specs += [row(nn), row(nr)]
        out_shape += [jax.ShapeDtypeStruct((b, s, nn), BF16), jax.ShapeDtypeStruct((b, s, nr), BF16)]
    return pl.pallas_call(
        functools.partial(_proj_kernel, prompt=prompt),
        grid=(b, s // tm),
        in_specs=in_specs, out_specs=out_specs, out_shape=out_shape,
        compiler_params=_params("parallel", "parallel"),
        name="proj_prompt" if prompt else "proj_sample",
    )(*args)


def _softplus(x):
    return jnp.maximum(x, 0.0) + jnp.log1p(jnp.exp(-jnp.abs(x)))


def _sigmoid(z):
    return 0.5 * jnp.tanh(0.5 * z) + 0.5


def _lru_coeffs(xc, wbd_ref, bab_ref, lam_ref):
    z = _dot(xc.astype(BF16), wbd_ref[...]) + bab_ref[...]
    r = _sigmoid(z[:, :D_RNN])
    gi = _sigmoid(z[:, D_RNN:])
    log_a = -LRU_C * r * _softplus(-lam_ref[...])
    a = jnp.exp(log_a)
    th = jnp.tanh(log_a)
    mult = jnp.sqrt(-2.0 * th / (1.0 - th))
    return a, mult * (gi * xc)


SUBLANES = 8


def _shift_rows(x, tail, k):
    rolled = pltpu.roll(x, k, axis=0)
    head_rows = lax.broadcasted_iota(jnp.int32, tail.shape, 0)
    first = jnp.where(head_rows < k, pltpu.roll(tail, k, axis=0), rolled[:SUBLANES])
    return jnp.concatenate([first, rolled[SUBLANES:]], axis=0)


def _rglru_prompt_kernel(xr_ref, gy_ref, cw_ref, cb_ref, wbd_ref, bab_ref, lam_ref,
                         hg_ref, hlast_ref, convnew_ref, tail_s, a_s, b_s, h_s, hcar, *, ts):
    t = pl.program_id(0)
    nb = xr_ref.shape[0]

    @pl.when(t == 0)
    def _():
        tail_s[...] = jnp.zeros(tail_s.shape, F32)
        hcar[...] = jnp.zeros(hcar.shape, F32)

    for bi in range(nb):
        x = xr_ref[bi]
        tail = tail_s[bi]
        xc = cb_ref[...] + _shift_rows(x, tail, 3) * cw_ref[0:1, :]
        xc = xc + _shift_rows(x, tail, 2) * cw_ref[1:2, :]
        xc = xc + _shift_rows(x, tail, 1) * cw_ref[2:3, :]
        xc = xc + x * cw_ref[3:4, :]
        tail_s[bi] = x[ts - SUBLANES:, :]
        a, b = _lru_coeffs(xc, wbd_ref, bab_ref, lam_ref)
        a_s[bi] = a
        b_s[bi] = b

    def step(i, hs):
        out = []
        for bi in range(nb):
            h = a_s[bi, pl.ds(i, 1), :] * hs[bi] + b_s[bi, pl.ds(i, 1), :]
            h_s[bi, pl.ds(i, 1), :] = h
            out.append(h)
        return tuple(out)

    hs = lax.fori_loop(0, ts, step, tuple(hcar[bi] for bi in range(nb)), unroll=8)
    for bi in range(nb):
        hcar[bi] = hs[bi]
        hg_ref[bi] = (h_s[bi] * gy_ref[bi].astype(F32)).astype(BF16)

    @pl.when(t == pl.num_programs(0) - 1)
    def _():
        for bi in range(nb):
            hlast_ref[bi] = hs[bi]
            convnew_ref[bi] = xr_ref[bi, ts - (CONV_W - 1):, :]


def _rglru_prompt(xr, gy, w, *, ts):
    b, s, _ = xr.shape
    rows = pl.BlockSpec((b, ts, D_RNN), lambda t: (0, t, 0))
    whole = lambda d: pl.BlockSpec((b, d, D_RNN), lambda t: (0, 0, 0))
    return pl.pallas_call(
        functools.partial(_rglru_prompt_kernel, ts=ts),
        grid=(s // ts,),
        in_specs=[rows, rows, _resident((CONV_W, D_RNN)), _resident((1, D_RNN)), _resident((D_RNN, 2 * D_RNN)),
                  _resident((1, 2 * D_RNN)), _resident((1, D_RNN))],
        out_specs=[rows, whole(1), whole(CONV_W - 1)],
        out_shape=[jax.ShapeDtypeStruct((b, s, D_RNN), BF16), jax.ShapeDtypeStruct((b, 1, D_RNN), F32),
                   jax.ShapeDtypeStruct((b, CONV_W - 1, D_RNN), F32)],
        scratch_shapes=[pltpu.VMEM((b, SUBLANES, D_RNN), F32), pltpu.VMEM((b, ts, D_RNN), F32),
                        pltpu.VMEM((b, ts, D_RNN), F32), pltpu.VMEM((b, ts, D_RNN), F32),
                        pltpu.VMEM((b, 1, D_RNN), F32)],
        compiler_params=_params("arbitrary"),
        name="rglru_prompt",
    )(xr, gy, w["conv_w"], w["conv_b"], w["w_lru"], w["b_lru"], w["lam"])


def _rglru_sample_kernel(xr_ref, gy_ref, sc_ref, h0_ref, cw_ref, cb_ref, wbd_ref, bab_ref, lam_ref,
                         hg_ref, hnew_ref, convnew_ref):
    x = xr_ref[...]
    xc = cb_ref[...] + sc_ref[0] * cw_ref[0:1, :]
    xc = xc + sc_ref[1] * cw_ref[1:2, :]
    xc = xc + sc_ref[2] * cw_ref[2:3, :]
    xc = xc + x * cw_ref[3:4, :]
    a, b = _lru_coeffs(xc, wbd_ref, bab_ref, lam_ref)
    h = a * h0_ref[...] + b
    hnew_ref[...] = h
    hg_ref[...] = (h * gy_ref[...].astype(F32)).astype(BF16)
    convnew_ref[0] = sc_ref[1]
    convnew_ref[1] = sc_ref[2]
    convnew_ref[2] = x


def _rglru_sample(xr, gy, sc, h0, w):
    n = xr.shape[0]
    return pl.pallas_call(
        _rglru_sample_kernel,
        out_shape=[jax.ShapeDtypeStruct((n, D_RNN), BF16), jax.ShapeDtypeStruct((n, D_RNN), F32),
                   jax.ShapeDtypeStruct((CONV_W - 1, n, D_RNN), F32)],
        compiler_params=pltpu.CompilerParams(vmem_limit_bytes=VMEM_LIMIT_BYTES),
        name="rglru_sample",
    )(xr, gy, sc, h0, w["conv_w"], w["conv_b"], w["w_lru"], w["b_lru"], w["lam"])


HEADS_PER_STEP = 4


def _attn_prompt_kernel(qt_ref, k_ref, vt_ref, o_ref, m_s, acc_s, s_a, s_b, *, t):
    qq = pl.program_id(2)
    heads = range(HEADS_PER_STEP)
    m_s[...] = jnp.full(m_s.shape, -jnp.inf, F32)
    acc_s[...] = jnp.zeros(acc_s.shape, F32)

    def rows(hh):
        return slice(hh * HEAD_PAD, (hh + 1) * HEAD_PAD)

    def scores(stage, j, hh):
        k0 = pl.multiple_of(j * t, t)
        return _dot(k_ref[0, hh, pl.ds(k0, t), :], qt_ref[0, rows(hh), stage * t:(stage + 1) * t])

    def consume(stage, s, j, hh, masked=False):
        k0 = pl.multiple_of(j * t, t)
        if masked:
            kpos = lax.broadcasted_iota(jnp.int32, (t, t), 0)
            qpos = lax.broadcasted_iota(jnp.int32, (t, t), 1)
            s = jnp.where(kpos <= qpos, s, -jnp.inf)
        m_old = m_s[stage, hh]
        m_new = jnp.maximum(m_old, jnp.max(s, axis=0, keepdims=True))
        p = jnp.exp2(s - m_new).astype(BF16)
        acc_s[stage, hh] = (jnp.exp2(m_old - m_new) * acc_s[stage, hh]
                            + _dot(vt_ref[0, rows(hh), pl.ds(k0, t)], p))
        m_s[stage, hh] = m_new

    def step(stage, j, src, dst, nxt):
        for hh in heads:
            dst[hh] = scores(nxt[0], nxt[1], hh)
            consume(stage, src[hh], j, hh)

    def finish(stage):
        o_t = jnp.concatenate(
            [acc_s[stage, hh][:V_HEAD] * (1.0 / acc_s[stage, hh][V_HEAD:V_HEAD + 1]) for hh in heads], axis=0)
        o_ref[0, stage * t:(stage + 1) * t, :] = o_t.T.astype(BF16)

    for hh in heads:
        s_a[hh] = scores(0, 0, hh)

    def pair0(i, c):
        step(0, 2 * i, s_a, s_b, (0, 2 * i + 1))
        step(0, 2 * i + 1, s_b, s_a, (0, 2 * i + 2))
        return c

    lax.fori_loop(0, qq, pair0, 0)
    for hh in heads:
        s_b[hh] = scores(1, 0, hh)
        consume(0, s_a[hh], 2 * qq, hh, masked=True)
    finish(0)

    def pair1(i, c):
        step(1, 2 * i, s_b, s_a, (1, 2 * i + 1))
        step(1, 2 * i + 1, s_a, s_b, (1, 2 * i + 2))
        return c

    lax.fori_loop(0, qq, pair1, 0)
    step(1, 2 * qq, s_b, s_a, (1, 2 * qq + 1))
    for hh in heads:
        consume(1, s_a[hh], 2 * qq + 1, hh, masked=True)
    finish(1)


def _attn_prompt(qt, k, vt, *, t):
    b, h, s, _ = k.shape
    hps = HEADS_PER_STEP
    assert (hps * V_HEAD) % LANES == 0 and h % hps == 0 and s % (2 * t) == 0
    return pl.pallas_call(
        functools.partial(_attn_prompt_kernel, t=t),
        grid=(b, h // hps, s // (2 * t)),
        in_specs=[pl.BlockSpec((1, hps * HEAD_PAD, 2 * t), lambda bi, hi, i: (bi, hi, i)),
                  pl.BlockSpec((1, hps, s, HEAD_PAD), lambda bi, hi, i: (bi, hi, 0, 0)),
                  pl.BlockSpec((1, hps * HEAD_PAD, s), lambda bi, hi, i: (bi, hi, 0))],
        out_specs=pl.BlockSpec((1, 2 * t, hps * V_HEAD), lambda bi, hi, i: (bi, i, hi)),
        out_shape=jax.ShapeDtypeStruct((b, s, h * V_HEAD), BF16),
        scratch_shapes=[pltpu.VMEM((2, hps, 1, t), F32), pltpu.VMEM((2, hps, HEAD_PAD, t), F32),
                        pltpu.VMEM((hps, t, t), F32), pltpu.VMEM((hps, t, t), F32)],
        compiler_params=_params("parallel", "parallel", "arbitrary"),
        name="attn_prompt",
    )(qt, k, vt)


def _absorb_kernel(qn_ref, wukt_ref, qa_ref):
    qn = qn_ref[...]
    for h in range(N_HEADS):
        qa_ref[:, h * KV_LORA:(h + 1) * KV_LORA] = _dot(
            qn[:, h * QK_NOPE:(h + 1) * QK_NOPE], wukt_ref[h]).astype(BF16)


def _absorb(qn, wukt3):
    n = qn.shape[0]
    return pl.pallas_call(
        _absorb_kernel,
        out_shape=jax.ShapeDtypeStruct((n, N_HEADS * KV_LORA), BF16),
        compiler_params=pltpu.CompilerParams(vmem_limit_bytes=VMEM_LIMIT_BYTES),
        name="absorb",
    )(qn, wukt3)


def _decode_kernel(pt_ref, qa_ref, qr_ref, cn_ref, kn_ref, pool_ckv, pool_krt, o_ref,
                   ckv_buf, krt_buf, page_a, page_b, p_a, p_b, tail_a, tail_b, l_a, l_b, sems, *, n_seq, n_pages):
    b = pl.program_id(0)
    slot = lax.rem(b, 2)
    set_a = (page_a, p_a, tail_a, l_a)
    set_b = (page_b, p_b, tail_b, l_b)

    def page_copies(page, sl, p):
        return (pltpu.make_async_copy(pool_ckv.at[page], ckv_buf.at[sl, p], sems.at[sl, 0]),
                pltpu.make_async_copy(pool_krt.at[page], krt_buf.at[sl, p], sems.at[sl, 1]))

    def start_row(row, sl):
        def body(p, c):
            for cp in page_copies(pt_ref[row * n_pages + p], sl, p):
                cp.start()
            return c
        lax.fori_loop(0, n_pages, body, 0)

    @pl.when(b == 0)
    def _():
        start_row(0, 0)
        page_b[...] = jnp.zeros(page_b.shape, BF16)
        p_b[...] = jnp.zeros(p_b.shape, BF16)
        tail_b[...] = jnp.zeros(tail_b.shape, F32)
        l_b[...] = jnp.ones(l_b.shape, F32)

    @pl.when(b + 1 < n_seq)
    def _():
        start_row(b + 1, 1 - slot)

    @pl.when(b < n_seq)
    def _():
        for p in range(n_pages):
            for cp in page_copies(0, slot, p):
                cp.wait()

    def score(cur):
        page_c, p_c, tail_c, l_c = cur
        qa = qa_ref[0]
        qr = qr_ref[0]
        s_pairs = []
        for p in range(0, n_pages, 2):
            pair = jnp.concatenate([ckv_buf[slot, p], ckv_buf[slot, p + 1]], axis=0).astype(BF16)
            page_c[p] = pair[:PAGE_SIZE]
            page_c[p + 1] = pair[PAGE_SIZE:]
            kr_pair = jnp.concatenate([krt_buf[slot, p], krt_buf[slot, p + 1]], axis=1).astype(BF16)
            s_pairs.append(_dot_nt(qa, pair) + _dot(qr, kr_pair))
        s = jnp.concatenate(s_pairs, axis=1)
        cn = cn_ref[0].astype(BF16).astype(F32)
        kn = kn_ref[0].astype(BF16).astype(F32)
        s_new = (jnp.sum(qa.astype(F32) * cn, axis=-1, keepdims=True)
                 + jnp.sum(qr.astype(F32) * kn, axis=-1, keepdims=True))
        m = jnp.maximum(jnp.max(s, axis=-1, keepdims=True), s_new)
        p_past = jnp.exp2(s - m)
        p_new = jnp.exp2(s_new - m)
        l_c[...] = jnp.sum(p_past, axis=-1, keepdims=True) + p_new
        tail_c[...] = p_new * cn
        p_c[...] = p_past.astype(BF16)

    def attend(prv):
        page_p, p_p, tail_p, l_p = prv
        acc = tail_p[...]
        for p in range(n_pages):
            acc = acc + _dot(p_p[:, p * PAGE_SIZE:(p + 1) * PAGE_SIZE], page_p[p])
        o_ref[0] = (acc / l_p[...]).astype(BF16)

    @pl.when(jnp.logical_and(b < n_seq, slot == 0))
    def _():
        score(set_a)
        attend(set_b)

    @pl.when(jnp.logical_and(b < n_seq, slot == 1))
    def _():
        score(set_b)
        attend(set_a)

    @pl.when(b == n_seq)
    def _():
        attend(set_a if (n_seq - 1) % 2 == 0 else set_b)


def _decode(page_table, qa, qr, ckv_new, kr_new, pool_ckv, pool_krt):
    n, n_pages = page_table.shape
    per_row = lambda d0, d1: pl.BlockSpec((1, d0, d1), lambda bi, pt: (jnp.minimum(bi, n - 1), 0, 0))
    hbm = pl.BlockSpec(memory_space=pl.ANY)
    page_set = [pltpu.VMEM((n_pages, PAGE_SIZE, KV_LORA), BF16)] * 2
    grid_spec = pltpu.PrefetchScalarGridSpec(
        num_scalar_prefetch=1,
        grid=(n + 1,),
        in_specs=[per_row(N_HEADS, KV_LORA), per_row(N_HEADS, QK_ROPE), per_row(1, KV_LORA), per_row(1, QK_ROPE),
                  hbm, hbm],
        out_specs=pl.BlockSpec((1, N_HEADS, KV_LORA), lambda bi, pt: (jnp.maximum(bi - 1, 0), 0, 0)),
        scratch_shapes=[pltpu.VMEM((2, n_pages, PAGE_SIZE, KV_LORA), F32),
                        pltpu.VMEM((2, n_pages, QK_ROPE, PAGE_SIZE), F32),
                        *page_set,
                        *([pltpu.VMEM((N_HEADS, n_pages * PAGE_SIZE), BF16)] * 2),
                        *([pltpu.VMEM((N_HEADS, KV_LORA), F32)] * 2),
                        *([pltpu.VMEM((N_HEADS, 1), F32)] * 2),
                        pltpu.SemaphoreType.DMA((2, 2))],
    )
    return pl.pallas_call(
        functools.partial(_decode_kernel, n_seq=n, n_pages=n_pages),
        grid_spec=grid_spec,
        out_shape=jax.ShapeDtypeStruct((n, N_HEADS, KV_LORA), BF16),
        compiler_params=_params("arbitrary"),
        name="decode",
    )(page_table.reshape(-1), qa, qr, ckv_new, kr_new, pool_ckv, pool_krt)


def _unabsorb_kernel(ol_ref, wuv_ref, o_ref):
    for h2 in range(N_HEADS // 2):
        pair = jnp.zeros((ol_ref.shape[0], LANES), F32)
        for h in (2 * h2, 2 * h2 + 1):
            pair = pair + _dot(ol_ref[:, h * KV_LORA:(h + 1) * KV_LORA], wuv_ref[h])
        o_ref[:, h2 * LANES:(h2 + 1) * LANES] = pair.astype(BF16)


def _unabsorb(o_lat, w_uv_pair):
    n = o_lat.shape[0]
    return pl.pallas_call(
        _unabsorb_kernel,
        out_shape=jax.ShapeDtypeStruct((n, N_HEADS * V_HEAD), BF16),
        compiler_params=pltpu.CompilerParams(vmem_limit_bytes=VMEM_LIMIT_BYTES),
        name="unabsorb",
    )(o_lat, w_uv_pair)


def _back_kernel(x_ref, hg_ref, o_ref, ga_ref, gb_ref, p_ref, wrnn_ref, wattn_ref, wout_ref, mpost_ref,
                 fpre_ref, wg_ref, wu_ref, wd_ref, fpost_ref, pg_ref, pp_ref, ppost_ref, y_ref, *, f_chunk):
    y_a = _dot(hg_ref[...], wrnn_ref[...])
    y_b = _dot(o_ref[...], wattn_ref[...])
    m = ga_ref[...].astype(F32) * y_a + gb_ref[...].astype(F32) * y_b
    x = x_ref[...] + _rms(_dot(m.astype(BF16), wout_ref[...]), mpost_ref[...])
    x = _ffn_block(x, fpre_ref, wg_ref, wu_ref, wd_ref, fpost_ref, f_chunk)
    e = jax.nn.sigmoid(_dot(x.astype(BF16), pg_ref[...])) * _dot(p_ref[...].astype(BF16), pp_ref[...])
    y_ref[...] = x + _rms(e, ppost_ref[...])


def _back(x, hg, o, ga, gb, p, w, *, tm):
    n = x.shape[0]
    row = lambda d: pl.BlockSpec((tm, d), lambda i: (i, 0))
    weights = [w["w_branch_rnn"], w["w_branch_attn"], w["w_out"], w["mix_post"], *w["ffn2"],
               w["ple_gate"], w["ple_proj"], w["ple_post"]]
    return pl.pallas_call(
        functools.partial(_back_kernel, f_chunk=D_FF // 2),
        grid=(n // tm,),
        in_specs=[row(D_MODEL), row(D_RNN), row(D_MODEL), row(D_MODEL), row(D_MODEL), row(D_PLE)]
                 + [_resident(a.shape) for a in weights],
        out_specs=row(D_MODEL),
        out_shape=jax.ShapeDtypeStruct((n, D_MODEL), F32),
        compiler_params=_params("parallel"),
        name="back",
    )(x, hg, o, ga, gb, p, *weights)


def _rot_cols(w):
    half = QK_ROPE // 2
    return jnp.concatenate([-w[..., half:], w[..., :half]], axis=-1)


def _pad_axis(a, axis, before, after):
    pads = [(0, 0)] * a.ndim
    pads[axis] = (before, after)
    return jnp.pad(a, pads)


def _prep_layer(i, ffn1_pre, ffn1_w_gate, ffn1_w_up, ffn1_w_down, ffn1_post, mix_pre, w_in, conv_w, conv_b,
                lru_w_a, lru_b_a, lru_w_i, lru_b_i, lru_lambda, w_branch_rnn, q_norm, w_uq, w_qr, kv_norm, w_uk,
                w_uv, w_branch_attn, w_out, mix_post, ffn2_pre, ffn2_w_gate, ffn2_w_up, ffn2_w_down, ffn2_post,
                ple_gate, ple_proj, ple_post):
    vec = lambda a: a[i].reshape(1, -1)
    w_in_i = w_in[i]
    w_kr = w_in_i[:, O_KV:O_KR]
    pad_r = HEAD_PAD - QK_ROPE - QK_NOPE
    w_qr_i, w_uq_i = w_qr[i], w_uq[i]
    same_block = (jnp.arange(D_RNN)[:, None] // RNN_BLOCK) == (jnp.arange(D_RNN)[None, :] // RNN_BLOCK)
    blockdiag = lambda wb: jnp.where(same_block, jnp.tile(wb.reshape(D_RNN, RNN_BLOCK), (1, RNN_BLOCKS)), 0.0)
    return {
        "ffn1": (vec(ffn1_pre), ffn1_w_gate[i].astype(BF16), ffn1_w_up[i].astype(BF16),
                 ffn1_w_down[i].astype(BF16), vec(ffn1_post)),
        "ffn2": (vec(ffn2_pre), ffn2_w_gate[i].astype(BF16), ffn2_w_up[i].astype(BF16),
                 ffn2_w_down[i].astype(BF16), vec(ffn2_post)),
        "mix_pre": vec(mix_pre),
        "w_main": jnp.concatenate([w_in_i[:, :O_KV], w_in_i[:, O_KR:]], axis=1).astype(BF16),
        "w_kr2": jnp.concatenate([_pad_axis(w_kr, 1, 0, HEAD_PAD - QK_ROPE),
                                  _pad_axis(_rot_cols(w_kr), 1, 0, HEAD_PAD - QK_ROPE)], axis=1).astype(BF16),
        "q_norm": vec(q_norm), "kv_norm": vec(kv_norm),
        "w_qrt": jnp.transpose(w_qr_i, (1, 2, 0)).reshape(N_HEADS * QK_ROPE, Q_LORA).astype(BF16),
        "w_qrt_rot": jnp.transpose(_rot_cols(w_qr_i), (1, 2, 0)).reshape(N_HEADS * QK_ROPE, Q_LORA).astype(BF16),
        "w_uqt": jnp.transpose(w_uq_i, (1, 2, 0)).reshape(N_HEADS * QK_NOPE, Q_LORA).astype(BF16),
        "w_uk_pad": _pad_axis(w_uk[i], 2, QK_ROPE, pad_r).reshape(KV_LORA, N_HEADS * HEAD_PAD).astype(BF16),
        "w_uvt": jnp.transpose(w_uv[i], (1, 2, 0)).reshape(N_HEADS * V_HEAD, KV_LORA).astype(BF16),
        "w_uv_pair": jnp.transpose(
            jnp.where((jnp.arange(N_HEADS) % 2 == 1)[None, :, None], _pad_axis(w_uv[i], 2, V_HEAD, 0),
                      _pad_axis(w_uv[i], 2, 0, V_HEAD)), (1, 0, 2)).astype(BF16),
        "w_uq": w_uq_i.reshape(Q_LORA, N_HEADS * QK_NOPE).astype(BF16),
        "w_qr": w_qr_i.reshape(Q_LORA, N_HEADS * QK_ROPE).astype(BF16),
        "w_qr_rot": _rot_cols(w_qr_i).reshape(Q_LORA, N_HEADS * QK_ROPE).astype(BF16),
        "w_ukt3": jnp.transpose(w_uk[i], (1, 2, 0)).astype(BF16),
        "conv_w": conv_w[i], "conv_b": vec(conv_b),
        "w_lru": jnp.concatenate([blockdiag(lru_w_a[i]), blockdiag(lru_w_i[i])], axis=1).astype(BF16),
        "b_lru": jnp.concatenate([lru_b_a[i].reshape(1, -1), lru_b_i[i].reshape(1, -1)], axis=1),
        "lam": vec(lru_lambda),
        "w_branch_rnn": w_branch_rnn[i].astype(BF16),
        "w_branch_attn": w_branch_attn[i].astype(BF16),
        "w_out": w_out[i].astype(BF16), "mix_post": vec(mix_post),
        "ple_gate": ple_gate[i].astype(BF16), "ple_proj": ple_proj[i].astype(BF16), "ple_post": vec(ple_post),
    }


def _rope_tables(pos, *, prompt):
    half = QK_ROPE // 2
    freqs = ROPE_THETA ** (-jnp.arange(half, dtype=F32) / half)
    ang = pos.astype(F32)[:, None] * freqs[None, :]
    cos32 = jnp.tile(jnp.cos(ang), (1, 2))
    sin32 = jnp.tile(jnp.sin(ang), (1, 2))
    n = pos.shape[0]
    tabs = {"cos_k": _pad_axis(cos32, 1, 0, HEAD_PAD - QK_ROPE), "sin_k": _pad_axis(sin32, 1, 0, HEAD_PAD - QK_ROPE)}
    if prompt:
        tabs["cos_qt"] = Q_SCALE * cos32.T
        tabs["sin_qt"] = Q_SCALE * sin32.T
    else:
        tabs["cos512"] = Q_SCALE * jnp.tile(cos32, (1, N_HEADS))
        tabs["sin512"] = Q_SCALE * jnp.tile(sin32, (1, N_HEADS))
    return tabs


def _tile(n, pref):
    return pref if n % pref == 0 else n


def _layer_prompt(x, p, w):
    b, s, _ = x.shape
    n = b * s
    tm = _tile(n, 512)
    tabs = _rope_tables(jnp.arange(s, dtype=jnp.int32), prompt=True)
    x1 = _ffn(x.reshape(n, D_MODEL), *w["ffn1"], tm=tm)
    xr, gy, ckv, krope, ga, gb, qt, k, vt = _proj(x1.reshape(b, s, D_MODEL), w, tabs, prompt=True, tm=_tile(s, 256))
    hg, h_last, conv_new = _rglru_prompt(xr, gy, w, ts=_tile(s, 256))
    o = _attn_prompt(qt, k, vt, t=_tile(s, 512))
    y = _back(x1, hg.reshape(n, D_RNN), o.reshape(n, D_MODEL), ga.reshape(n, D_MODEL), gb.reshape(n, D_MODEL),
              p.reshape(n, D_PLE), w, tm=_tile(n, 256))
    return y.reshape(b, s, D_MODEL), (ckv, krope, h_last.reshape(b, D_RNN), conv_new)


def _layer_sample(x, p, h0, conv_buf, pool_ckv, pool_kr, page_table, w):
    n, s, _ = x.shape
    past_len = page_table.shape[1] * PAGE_SIZE
    tabs = _rope_tables(jnp.full((n,), past_len, jnp.int32), prompt=False)
    x1 = _ffn(x.reshape(n, D_MODEL), *w["ffn1"], tm=n)
    xr, gy, ckv, krope, ga, gb, qn, qr = _proj(x1.reshape(1, n, D_MODEL), w, tabs, prompt=False, tm=n)
    hg, h_new, conv_new = _rglru_sample(xr[0], gy[0], jnp.transpose(conv_buf, (1, 0, 2)), h0, w)
    qa = _absorb(qn[0], w["w_ukt3"]).reshape(n, N_HEADS, KV_LORA)
    o_lat = _decode(page_table, qa, qr.reshape(n, N_HEADS, QK_ROPE), ckv.reshape(n, 1, KV_LORA),
                    krope.reshape(n, 1, QK_ROPE), pool_ckv, jnp.transpose(pool_kr, (0, 2, 1)))
    o = _unabsorb(o_lat.reshape(n, N_HEADS * KV_LORA), w["w_uv_pair"])
    y = _back(x1, hg, o, ga[0], gb[0], p.reshape(n, D_PLE), w, tm=n)
    return (y.reshape(n, s, D_MODEL),
            (ckv.reshape(n, s, KV_LORA), krope.reshape(n, s, QK_ROPE), h_new, jnp.transpose(conv_new, (1, 0, 2))))


def kernel(x_prompt, x_sample, p_prompt, p_sample, cache_ckv, cache_krope, state_h, state_conv, page_table,
           ffn1_pre, ffn1_w_gate, ffn1_w_up, ffn1_w_down, ffn1_post, mix_pre, w_in, conv_w, conv_b, lru_w_a,
           lru_b_a, lru_w_i, lru_b_i, lru_lambda, w_branch_rnn, q_norm, w_uq, w_qr, kv_norm, w_uk, w_uv,
           w_branch_attn, w_out, mix_post, ffn2_pre, ffn2_w_gate, ffn2_w_up, ffn2_w_down, ffn2_post, ple_gate,
           ple_proj, ple_post):
    assert x_sample.shape[1] == 1, "the sample group carries one new token per sequence"
    depth = ffn1_pre.shape[0]
    hp, hs = x_prompt, x_sample
    st_p, st_s = [], []
    for i in range(depth):
        w = _prep_layer(i, ffn1_pre, ffn1_w_gate, ffn1_w_up, ffn1_w_down, ffn1_post, mix_pre, w_in, conv_w, conv_b,
                        lru_w_a, lru_b_a, lru_w_i, lru_b_i, lru_lambda, w_branch_rnn, q_norm, w_uq, w_qr, kv_norm,
                        w_uk, w_uv, w_branch_attn, w_out, mix_post, ffn2_pre, ffn2_w_gate, ffn2_w_up, ffn2_w_down,
                        ffn2_post, ple_gate, ple_proj, ple_post)
        hp, sp = _layer_prompt(hp, p_prompt[i], w)
        hs, ss = _layer_sample(hs, p_sample[i], state_h[i], state_conv[i], cache_ckv[i], cache_krope[i],
                               page_table, w)
        st_p.append(sp)
        st_s.append(ss)
    stack = lambda sts, k: jnp.stack([s[k] for s in sts])
    return (hp, hs, stack(st_p, 0), stack(st_p, 1), stack(st_p, 2), stack(st_p, 3),
            stack(st_s, 0), stack(st_s, 1), stack(st_s, 2), stack(st_s, 3))
```

```python
import functools

import jax
import jax.numpy as jnp
from jax import lax
from jax.experimental import pallas as pl
from jax.experimental.pallas import tpu as pltpu

D_MODEL = 1024
D_RNN = 1280
RNN_BLOCKS = 16
RNN_BLOCK = D_RNN // RNN_BLOCKS
CONV_W = 4
LRU_C = 8.0
N_HEADS = 16
QK_NOPE = 64
QK_ROPE = 32
V_HEAD = 64
Q_LORA = 384
KV_LORA = 256
ROPE_THETA = 10000.0
SM_SCALE = (QK_NOPE + QK_ROPE) ** -0.5
Q_SCALE = SM_SCALE * 1.4426950408889634
D_FF = 2816
D_PLE = 256
EPS = 1e-6
PAGE_SIZE = 128
O_XR = D_RNN
O_YR = O_XR + D_RNN
O_Q = O_YR + Q_LORA
O_KV = O_Q + KV_LORA
O_KR = O_KV + QK_ROPE

LANES = 128
HEAD_PAD = LANES
VMEM_LIMIT_BYTES = 56 * 1024 * 1024

F32 = jnp.float32
BF16 = jnp.bfloat16


def _rms(x, g):
    return x * lax.rsqrt(jnp.mean(x * x, axis=-1, keepdims=True) + EPS) * g


def _dot(a, b):
    return jnp.dot(a, b, preferred_element_type=F32)


def _dot_nt(a, b):
    return lax.dot_general(a, b, (((1,), (1,)), ((), ())), preferred_element_type=F32)


def _resident(shape):
    nd = len(shape)
    return pl.BlockSpec(shape, lambda *_: (0,) * nd, pipeline_mode=pl.Buffered(1))


def _params(*sem):
    return pltpu.CompilerParams(dimension_semantics=sem, vmem_limit_bytes=VMEM_LIMIT_BYTES)


def _ffn_block(x, pre_ref, wg_ref, wu_ref, wd_ref, post_ref, f_chunk):
    u = _rms(x, pre_ref[...]).astype(BF16)
    acc = jnp.zeros(x.shape, F32)
    for c in range(D_FF // f_chunk):
        sl = slice(c * f_chunk, (c + 1) * f_chunk)
        g = _dot(u, wg_ref[:, sl])
        h = (g * jax.nn.sigmoid(g)) * _dot(u, wu_ref[:, sl])
        acc = acc + _dot(h.astype(BF16), wd_ref[sl, :])
    return x + 0.5 * _rms(acc, post_ref[...])


N_MAIN = 2 * D_RNN + Q_LORA + KV_LORA + 2 * D_MODEL
M_YR = D_RNN
M_CQ = 2 * D_RNN
M_KV = M_CQ + Q_LORA
M_GA = M_KV + KV_LORA
M_GB = M_GA + D_MODEL


def _proj_kernel(*refs, prompt):
    (x_ref, fpre_ref, wg_ref, wu_ref, wd_ref, fpost_ref), refs = refs[:6], refs[6:]
    (pre_ref, wmain_ref, qn_ref, kvn_ref, wkr_ref, cosk_ref, sink_ref), refs = refs[:7], refs[7:]
    if prompt:
        (wqrt_ref, wqrtrot_ref, wuqt_ref, cosqt_ref, sinqt_ref, wuk_ref, wuvt_ref), refs = refs[:7], refs[7:]
    else:
        (wuq_ref, wqr_ref, wqrrot_ref, cos512_ref, sin512_ref), refs = refs[:5], refs[5:]
    (x1_ref, xr_ref, gy_ref, ckv_ref, kr_ref, ga_ref, gb_ref), refs = refs[:7], refs[7:]

    x1 = _ffn_block(x_ref[0], fpre_ref, wg_ref, wu_ref, wd_ref, fpost_ref, D_FF // 2)
    x1_ref[0] = x1
    u = _rms(x1, pre_ref[...]).astype(BF16)
    xr_ref[0] = _dot(u, wmain_ref[:, 0:M_YR])
    gy_ref[0] = jax.nn.gelu(_dot(u, wmain_ref[:, M_YR:M_CQ])).astype(BF16)
    ga_ref[0] = jax.nn.sigmoid(_dot(u, wmain_ref[:, M_GA:M_GB])).astype(BF16)
    gb_ref[0] = jax.nn.sigmoid(_dot(u, wmain_ref[:, M_GB:N_MAIN])).astype(BF16)
    cq = _rms(_dot(u, wmain_ref[:, M_CQ:M_KV]), qn_ref[...]).astype(BF16)
    ckv = _rms(_dot(u, wmain_ref[:, M_KV:M_GA]), kvn_ref[...])
    ckv_ref[0] = ckv
    ckv_b = ckv.astype(BF16)
    kr2 = _dot(u, wkr_ref[...])
    kr = kr2[:, :HEAD_PAD] * cosk_ref[...] + kr2[:, HEAD_PAD:] * sink_ref[...]
    kr_ref[0] = kr[:, :QK_ROPE]

    if prompt:
        qt_ref, k_ref, vt_ref = refs
        tm = cq.shape[0]
        k_nope = _dot(ckv_b, wuk_ref[...])
        qr_t = _dot_nt(wqrt_ref[...], cq)
        qrot_t = _dot_nt(wqrtrot_ref[...], cq)
        qn_t = _dot_nt(wuqt_ref[...], cq) * Q_SCALE
        v_t = _dot_nt(wuvt_ref[...], ckv_b)
        cos_t = cosqt_ref[...]
        sin_t = sinqt_ref[...]
        q_pad = jnp.zeros((HEAD_PAD - QK_ROPE - QK_NOPE, tm), BF16)
        ones_row = (lax.broadcasted_iota(jnp.int32, (HEAD_PAD - V_HEAD, tm), 0) == 0).astype(BF16)
        for h in range(N_HEADS):
            r0 = h * HEAD_PAD
            k_ref[0, h] = (k_nope[:, r0:r0 + HEAD_PAD] + kr).astype(BF16)
            rope = slice(h * QK_ROPE, (h + 1) * QK_ROPE)
            qt_ref[0, r0:r0 + QK_ROPE, :] = (qr_t[rope] * cos_t + qrot_t[rope] * sin_t).astype(BF16)
            qt_ref[0, r0 + QK_ROPE:r0 + QK_ROPE + QK_NOPE, :] = qn_t[h * QK_NOPE:(h + 1) * QK_NOPE].astype(BF16)
            qt_ref[0, r0 + QK_ROPE + QK_NOPE:r0 + HEAD_PAD, :] = q_pad
            vt_ref[0, r0:r0 + V_HEAD, :] = v_t[h * V_HEAD:(h + 1) * V_HEAD].astype(BF16)
            vt_ref[0, r0 + V_HEAD:r0 + HEAD_PAD, :] = ones_row
    else:
        qn_out_ref, qr_out_ref = refs
        qn_out_ref[0] = (_dot(cq, wuq_ref[...]) * Q_SCALE).astype(BF16)
        qr_out_ref[0] = (_dot(cq, wqr_ref[...]) * cos512_ref[...]
                         + _dot(cq, wqrrot_ref[...]) * sin512_ref[...]).astype(BF16)


def _proj(x, w, tabs, *, prompt, tm):
    b, s, _ = x.shape
    row = lambda d: pl.BlockSpec((1, tm, d), lambda bi, i: (bi, i, 0))
    tab = lambda d: pl.BlockSpec((tm, d), lambda bi, i: (i, 0))
    tab_t = pl.BlockSpec((QK_ROPE, tm), lambda bi, i: (0, i))
    in_specs = [row(D_MODEL), *[_resident(a.shape) for a in w["ffn1"]],
                _resident((1, D_MODEL)), _resident((D_MODEL, N_MAIN)), _resident((1, Q_LORA)),
                _resident((1, KV_LORA)), _resident((D_MODEL, 2 * HEAD_PAD)), tab(HEAD_PAD), tab(HEAD_PAD)]
    args = [x, *w["ffn1"], w["mix_pre"], w["w_main"], w["q_norm"], w["kv_norm"], w["w_kr2"],
            tabs["cos_k"], tabs["sin_k"]]
    out_specs = [row(D_MODEL), row(D_RNN), row(D_RNN), row(KV_LORA), row(QK_ROPE), row(D_MODEL), row(D_MODEL)]
    out_shape = [jax.ShapeDtypeStruct((b, s, D_MODEL), F32),
                 jax.ShapeDtypeStruct((b, s, D_RNN), F32), jax.ShapeDtypeStruct((b, s, D_RNN), BF16),
                 jax.ShapeDtypeStruct((b, s, KV_LORA), F32), jax.ShapeDtypeStruct((b, s, QK_ROPE), F32),
                 jax.ShapeDtypeStruct((b, s, D_MODEL), BF16), jax.ShapeDtypeStruct((b, s, D_MODEL), BF16)]
    hp = N_HEADS * HEAD_PAD
    if prompt:
        in_specs += [_resident((N_HEADS * QK_ROPE, Q_LORA)), _resident((N_HEADS * QK_ROPE, Q_LORA)),
                     _resident((N_HEADS * QK_NOPE, Q_LORA)), tab_t, tab_t,
                     _resident((KV_LORA, hp)), _resident((N_HEADS * V_HEAD, KV_LORA))]
        args += [w["w_qrt"], w["w_qrt_rot"], w["w_uqt"], tabs["cos_qt"], tabs["sin_qt"], w["w_uk_pad"], w["w_uvt"]]
        lanes_major = pl.BlockSpec((1, hp, tm), lambda bi, i: (bi, 0, i))
        out_specs += [lanes_major, pl.BlockSpec((1, N_HEADS, tm, HEAD_PAD), lambda bi, i: (bi, 0, i, 0)), lanes_major]
        out_shape += [jax.ShapeDtypeStruct((b, hp, s), BF16),
                      jax.ShapeDtypeStruct((b, N_HEADS, s, HEAD_PAD), BF16),
                      jax.ShapeDtypeStruct((b, hp, s), BF16)]
    else:
        nn, nr = N_HEADS * QK_NOPE, N_HEADS * QK_ROPE
        in_specs += [_resident((Q_LORA, nn)), _resident((Q_LORA, nr)), _resident((Q_LORA, nr)), tab(nr), tab(nr)]
        args += [w["w_uq"], w["w_qr"], w["w_qr_rot"], tabs["cos512"], tabs["sin512"]]
        out_specs += [row(nn), row(nr)]
        out_shape += [jax.ShapeDtypeStruct((b, s, nn), BF16), jax.ShapeDtypeStruct((b, s, nr), BF16)]
    return pl.pallas_call(
        functools.partial(_proj_kernel, prompt=prompt),
        grid=(b, s // tm),
        in_specs=in_specs, out_specs=out_specs, out_shape=out_shape,
        compiler_params=_params("parallel", "parallel"),
        name="proj_prompt" if prompt else "proj_sample",
    )(*args)


def _softplus(x):
    return jnp.maximum(x, 0.0) + jnp.log1p(jnp.exp(-jnp.abs(x)))


def _sigmoid(z):
    return 0.5 * jnp.tanh(0.5 * z) + 0.5


def _lru_coeffs(xc, wbd_ref, bab_ref, lam_ref):
    z = _dot(xc.astype(BF16), wbd_ref[...]) + bab_ref[...]
    r = _sigmoid(z[:, :D_RNN])
    gi = _sigmoid(z[:, D_RNN:])
    log_a = -LRU_C * r * _softplus(-lam_ref[...])
    a = jnp.exp(log_a)
    th = jnp.tanh(log_a)
    mult = jnp.sqrt(-2.0 * th / (1.0 - th))
    return a, mult * (gi * xc)


SUBLANES = 8


def _shift_rows(x, tail, k):
    rolled = pltpu.roll(x, k, axis=0)
    head_rows = lax.broadcasted_iota(jnp.int32, tail.shape, 0)
    first = jnp.where(head_rows < k, pltpu.roll(tail, k, axis=0), rolled[:SUBLANES])
    return jnp.concatenate([first, rolled[SUBLANES:]], axis=0)


def _rglru_prompt_kernel(xr_ref, gy_ref, cw_ref, cb_ref, wbd_ref, bab_ref, lam_ref,
                         hg_ref, hlast_ref, convnew_ref, tail_s, a_s, b_s, h_s, hcar, *, ts):
    t = pl.program_id(0)
    nb = xr_ref.shape[0]

    @pl.when(t == 0)
    def _():
        tail_s[...] = jnp.zeros(tail_s.shape, F32)
        hcar[...] = jnp.zeros(hcar.shape, F32)

    for bi in range(nb):
        x = xr_ref[bi]
        tail = tail_s[bi]
        xc = cb_ref[...] + _shift_rows(x, tail, 3) * cw_ref[0:1, :]
        xc = xc + _shift_rows(x, tail, 2) * cw_ref[1:2, :]
        xc = xc + _shift_rows(x, tail, 1) * cw_ref[2:3, :]
        xc = xc + x * cw_ref[3:4, :]
        tail_s[bi] = x[ts - SUBLANES:, :]
        a, b = _lru_coeffs(xc, wbd_ref, bab_ref, lam_ref)
        a_s[bi] = a
        b_s[bi] = b

    def step(i, hs):
        out = []
        for bi in range(nb):
            h = a_s[bi, pl.ds(i, 1), :] * hs[bi] + b_s[bi, pl.ds(i, 1), :]
            h_s[bi, pl.ds(i, 1), :] = h
            out.append(h)
        return tuple(out)

    hs = lax.fori_loop(0, ts, step, tuple(hcar[bi] for bi in range(nb)), unroll=8)
    for bi in range(nb):
        hcar[bi] = hs[bi]
        hg_ref[bi] = (h_s[bi] * gy_ref[bi].astype(F32)).astype(BF16)

    @pl.when(t == pl.num_programs(0) - 1)
    def _():
        for bi in range(nb):
            hlast_ref[bi] = hs[bi]
            convnew_ref[bi] = xr_ref[bi, ts - (CONV_W - 1):, :]


def _rglru_prompt(xr, gy, w, *, ts):
    b, s, _ = xr.shape
    rows = pl.BlockSpec((b, ts, D_RNN), lambda t: (0, t, 0))
    whole = lambda d: pl.BlockSpec((b, d, D_RNN), lambda t: (0, 0, 0))
    return pl.pallas_call(
        functools.partial(_rglru_prompt_kernel, ts=ts),
        grid=(s // ts,),
        in_specs=[rows, rows, _resident((CONV_W, D_RNN)), _resident((1, D_RNN)), _resident((D_RNN, 2 * D_RNN)),
                  _resident((1, 2 * D_RNN)), _resident((1, D_RNN))],
        out_specs=[rows, whole(1), whole(CONV_W - 1)],
        out_shape=[jax.ShapeDtypeStruct((b, s, D_RNN), BF16), jax.ShapeDtypeStruct((b, 1, D_RNN), F32),
                   jax.ShapeDtypeStruct((b, CONV_W - 1, D_RNN), F32)],
        scratch_shapes=[pltpu.VMEM((b, SUBLANES, D_RNN), F32), pltpu.VMEM((b, ts, D_RNN), F32),
                        pltpu.VMEM((b, ts, D_RNN), F32), pltpu.VMEM((b, ts, D_RNN), F32),
                        pltpu.VMEM((b, 1, D_RNN), F32)],
        compiler_params=_params("arbitrary"),
        name="rglru_prompt",
    )(xr, gy, w["conv_w"], w["conv_b"], w["w_lru"], w["b_lru"], w["lam"])


def _rglru_sample_kernel(xr_ref, gy_ref, sc_ref, h0_ref, cw_ref, cb_ref, wbd_ref, bab_ref, lam_ref,
                         hg_ref, hnew_ref, convnew_ref):
    x = xr_ref[...]
    xc = cb_ref[...] + sc_ref[0] * cw_ref[0:1, :]
    xc = xc + sc_ref[1] * cw_ref[1:2, :]
    xc = xc + sc_ref[2] * cw_ref[2:3, :]
    xc = xc + x * cw_ref[3:4, :]
    a, b = _lru_coeffs(xc, wbd_ref, bab_ref, lam_ref)
    h = a * h0_ref[...] + b
    hnew_ref[...] = h
    hg_ref[...] = (h * gy_ref[...].astype(F32)).astype(BF16)
    convnew_ref[0] = sc_ref[1]
    convnew_ref[1] = sc_ref[2]
    convnew_ref[2] = x


def _rglru_sample(xr, gy, sc, h0, w):
    n = xr.shape[0]
    return pl.pallas_call(
        _rglru_sample_kernel,
        out_shape=[jax.ShapeDtypeStruct((n, D_RNN), BF16), jax.ShapeDtypeStruct((n, D_RNN), F32),
                   jax.ShapeDtypeStruct((CONV_W - 1, n, D_RNN), F32)],
        compiler_params=pltpu.CompilerParams(vmem_limit_bytes=VMEM_LIMIT_BYTES),
        name="rglru_sample",
    )(xr, gy, sc, h0, w["conv_w"], w["conv_b"], w["w_lru"], w["b_lru"], w["lam"])


HEADS_PER_STEP = 4


def _attn_prompt_kernel(qt_ref, k_ref, vt_ref, o_ref, m_s, acc_s, s_a, s_b, *, t):
    qq = pl.program_id(2)
    heads = range(HEADS_PER_STEP)
    m_s[...] = jnp.full(m_s.shape, -jnp.inf, F32)
    acc_s[...] = jnp.zeros(acc_s.shape, F32)

    def rows(hh):
        return slice(hh * HEAD_PAD, (hh + 1) * HEAD_PAD)

    def scores(stage, j, hh):
        k0 = pl.multiple_of(j * t, t)
        return _dot(k_ref[0, hh, pl.ds(k0, t), :], qt_ref[0, rows(hh), stage * t:(stage + 1) * t])

    def consume(stage, s, j, hh, masked=False):
        k0 = pl.multiple_of(j * t, t)
        if masked:
            kpos = lax.broadcasted_iota(jnp.int32, (t, t), 0)
            qpos = lax.broadcasted_iota(jnp.int32, (t, t), 1)
            s = jnp.where(kpos <= qpos, s, -jnp.inf)
        m_old = m_s[stage, hh]
        m_new = jnp.maximum(m_old, jnp.max(s, axis=0, keepdims=True))
        p = jnp.exp2(s - m_new).astype(BF16)
        acc_s[stage, hh] = (jnp.exp2(m_old - m_new) * acc_s[stage, hh]
                            + _dot(vt_ref[0, rows(hh), pl.ds(k0, t)], p))
        m_s[stage, hh] = m_new

    def step(stage, j, src, dst, nxt):
        for hh in heads:
            dst[hh] = scores(nxt[0], nxt[1], hh)
            consume(stage, src[hh], j, hh)

    def finish(stage):
        o_t = jnp.concatenate(
            [acc_s[stage, hh][:V_HEAD] * (1.0 / acc_s[stage, hh][V_HEAD:V_HEAD + 1]) for hh in heads], axis=0)
        o_ref[0, stage * t:(stage + 1) * t, :] = o_t.T.astype(BF16)

    for hh in heads:
        s_a[hh] = scores(0, 0, hh)

    def pair0(i, c):
        step(0, 2 * i, s_a, s_b, (0, 2 * i + 1))
        step(0, 2 * i + 1, s_b, s_a, (0, 2 * i + 2))
        return c

    lax.fori_loop(0, qq, pair0, 0)
    for hh in heads:
        s_b[hh] = scores(1, 0, hh)
        consume(0, s_a[hh], 2 * qq, hh, masked=True)
    finish(0)

    def pair1(i, c):
        step(1, 2 * i, s_b, s_a, (1, 2 * i + 1))
        step(1, 2 * i + 1, s_a, s_b, (1, 2 * i + 2))
        return c

    lax.fori_loop(0, qq, pair1, 0)
    step(1, 2 * qq, s_b, s_a, (1, 2 * qq + 1))
    for hh in heads:
        consume(1, s_a[hh], 2 * qq + 1, hh, masked=True)
    finish(1)


def _attn_prompt(qt, k, vt, *, t):
    b, h, s, _ = k.shape
    hps = HEADS_PER_STEP
    assert (hps * V_HEAD) % LANES == 0 and h % hps == 0 and s % (2 * t) == 0
    return pl.pallas_call(
        functools.partial(_attn_prompt_kernel, t=t),
        grid=(b, h // hps, s // (2 * t)),
        in_specs=[pl.BlockSpec((1, hps * HEAD_PAD, 2 * t), lambda bi, hi, i: (bi, hi, i)),
                  pl.BlockSpec((1, hps, s, HEAD_PAD), lambda bi, hi, i: (bi, hi, 0, 0)),
                  pl.BlockSpec((1, hps * HEAD_PAD, s), lambda bi, hi, i: (bi, hi, 0))],
        out_specs=pl.BlockSpec((1, 2 * t, hps * V_HEAD), lambda bi, hi, i: (bi, i, hi)),
        out_shape=jax.ShapeDtypeStruct((b, s, h * V_HEAD), BF16),
        scratch_shapes=[pltpu.VMEM((2, hps, 1, t), F32), pltpu.VMEM((2, hps, HEAD_PAD, t), F32),
                        pltpu.VMEM((hps, t, t), F32), pltpu.VMEM((hps, t, t), F32)],
        compiler_params=_params("parallel", "parallel", "arbitrary"),
        name="attn_prompt",
    )(qt, k, vt)


def _absorb_kernel(qn_ref, wukt_ref, qa_ref):
    qn = qn_ref[...]
    for h in range(N_HEADS):
        qa_ref[:, h * KV_LORA:(h + 1) * KV_LORA] = _dot(
            qn[:, h * QK_NOPE:(h + 1) * QK_NOPE], wukt_ref[h]).astype(BF16)


def _absorb(qn, wukt3):
    n = qn.shape[0]
    return pl.pallas_call(
        _absorb_kernel,
        out_shape=jax.ShapeDtypeStruct((n, N_HEADS * KV_LORA), BF16),
        compiler_params=pltpu.CompilerParams(vmem_limit_bytes=VMEM_LIMIT_BYTES),
        name="absorb",
    )(qn, wukt3)


def _decode_kernel(pt_ref, qa_ref, qr_ref, cn_ref, kn_ref, pool_ckv, pool_krt, o_ref,
                   ckv_buf, krt_buf, page_a, page_b, p_a, p_b, tail_a, tail_b, l_a, l_b, sems, *, n_seq, n_pages):
    b = pl.program_id(0)
    slot = lax.rem(b, 2)
    set_a = (page_a, p_a, tail_a, l_a)
    set_b = (page_b, p_b, tail_b, l_b)

    def page_copies(page, sl, p):
        return (pltpu.make_async_copy(pool_ckv.at[page], ckv_buf.at[sl, p], sems.at[sl, 0]),
                pltpu.make_async_copy(pool_krt.at[page], krt_buf.at[sl, p], sems.at[sl, 1]))

    def start_row(row, sl):
        def body(p, c):
            for cp in page_copies(pt_ref[row * n_pages + p], sl, p):
                cp.start()
            return c
        lax.fori_loop(0, n_pages, body, 0)

    @pl.when(b == 0)
    def _():
        start_row(0, 0)
        page_b[...] = jnp.zeros(page_b.shape, BF16)
        p_b[...] = jnp.zeros(p_b.shape, BF16)
        tail_b[...] = jnp.zeros(tail_b.shape, F32)
        l_b[...] = jnp.ones(l_b.shape, F32)

    @pl.when(b + 1 < n_seq)
    def _():
        start_row(b + 1, 1 - slot)

    @pl.when(b < n_seq)
    def _():
        for p in range(n_pages):
            for cp in page_copies(0, slot, p):
                cp.wait()

    def score(cur):
        page_c, p_c, tail_c, l_c = cur
        qa = qa_ref[0]
        qr = qr_ref[0]
        s_pairs = []
        for p in range(0, n_pages, 2):
            pair = jnp.concatenate([ckv_buf[slot, p], ckv_buf[slot, p + 1]], axis=0).astype(BF16)
            page_c[p] = pair[:PAGE_SIZE]
            page_c[p + 1] = pair[PAGE_SIZE:]
            kr_pair = jnp.concatenate([krt_buf[slot, p], krt_buf[slot, p + 1]], axis=1).astype(BF16)
            s_pairs.append(_dot_nt(qa, pair) + _dot(qr, kr_pair))
        s = jnp.concatenate(s_pairs, axis=1)
        cn = cn_ref[0].astype(BF16).astype(F32)
        kn = kn_ref[0].astype(BF16).astype(F32)
        s_new = (jnp.sum(qa.astype(F32) * cn, axis=-1, keepdims=True)
                 + jnp.sum(qr.astype(F32) * kn, axis=-1, keepdims=True))
        m = jnp.maximum(jnp.max(s, axis=-1, keepdims=True), s_new)
        p_past = jnp.exp2(s - m)
        p_new = jnp.exp2(s_new - m)
        l_c[...] = jnp.sum(p_past, axis=-1, keepdims=True) + p_new
        tail_c[...] = p_new * cn
        p_c[...] = p_past.astype(BF16)

    def attend(prv):
        page_p, p_p, tail_p, l_p = prv
        acc = tail_p[...]
        for p in range(n_pages):
            acc = acc + _dot(p_p[:, p * PAGE_SIZE:(p + 1) * PAGE_SIZE], page_p[p])
        o_ref[0] = (acc / l_p[...]).astype(BF16)

    @pl.when(jnp.logical_and(b < n_seq, slot == 0))
    def _():
        score(set_a)
        attend(set_b)

    @pl.when(jnp.logical_and(b < n_seq, slot == 1))
    def _():
        score(set_b)
        attend(set_a)

    @pl.when(b == n_seq)
    def _():
        attend(set_a if (n_seq - 1) % 2 == 0 else set_b)


def _decode(page_table, qa, qr, ckv_new, kr_new, pool_ckv, pool_krt):
    n, n_pages = page_table.shape
    per_row = lambda d0, d1: pl.BlockSpec((1, d0, d1), lambda bi, pt: (jnp.minimum(bi, n - 1), 0, 0))
    hbm = pl.BlockSpec(memory_space=pl.ANY)
    page_set = [pltpu.VMEM((n_pages, PAGE_SIZE, KV_LORA), BF16)] * 2
    grid_spec = pltpu.PrefetchScalarGridSpec(
        num_scalar_prefetch=1,
        grid=(n + 1,),
        in_specs=[per_row(N_HEADS, KV_LORA), per_row(N_HEADS, QK_ROPE), per_row(1, KV_LORA), per_row(1, QK_ROPE),
                  hbm, hbm],
        out_specs=pl.BlockSpec((1, N_HEADS, KV_LORA), lambda bi, pt: (jnp.maximum(bi - 1, 0), 0, 0)),
        scratch_shapes=[pltpu.VMEM((2, n_pages, PAGE_SIZE, KV_LORA), F32),
                        pltpu.VMEM((2, n_pages, QK_ROPE, PAGE_SIZE), F32),
                        *page_set,
                        *([pltpu.VMEM((N_HEADS, n_pages * PAGE_SIZE), BF16)] * 2),
                        *([pltpu.VMEM((N_HEADS, KV_LORA), F32)] * 2),
                        *([pltpu.VMEM((N_HEADS, 1), F32)] * 2),
                        pltpu.SemaphoreType.DMA((2, 2))],
    )
    return pl.pallas_call(
        functools.partial(_decode_kernel, n_seq=n, n_pages=n_pages),
        grid_spec=grid_spec,
        out_shape=jax.ShapeDtypeStruct((n, N_HEADS, KV_LORA), BF16),
        compiler_params=_params("arbitrary"),
        name="decode",
    )(page_table.reshape(-1), qa, qr, ckv_new, kr_new, pool_ckv, pool_krt)


def _unabsorb_kernel(ol_ref, wuv_ref, o_ref):
    for h2 in range(N_HEADS // 2):
        pair = jnp.zeros((ol_ref.shape[0], LANES), F32)
        for h in (2 * h2, 2 * h2 + 1):
            pair = pair + _dot(ol_ref[:, h * KV_LORA:(h + 1) * KV_LORA], wuv_ref[h])
        o_ref[:, h2 * LANES:(h2 + 1) * LANES] = pair.astype(BF16)


def _unabsorb(o_lat, w_uv_pair):
    n = o_lat.shape[0]
    return pl.pallas_call(
        _unabsorb_kernel,
        out_shape=jax.ShapeDtypeStruct((n, N_HEADS * V_HEAD), BF16),
        compiler_params=pltpu.CompilerParams(vmem_limit_bytes=VMEM_LIMIT_BYTES),
        name="unabsorb",
    )(o_lat, w_uv_pair)


def _back_kernel(x_ref, hg_ref, o_ref, ga_ref, gb_ref, p_ref, wrnn_ref, wattn_ref, wout_ref, mpost_ref,
                 fpre_ref, wg_ref, wu_ref, wd_ref, fpost_ref, pg_ref, pp_ref, ppost_ref, y_ref, *, f_chunk):
    y_a = _dot(hg_ref[...], wrnn_ref[...])
    y_b = _dot(o_ref[...], wattn_ref[...])
    m = ga_ref[...].astype(F32) * y_a + gb_ref[...].astype(F32) * y_b
    x = x_ref[...] + _rms(_dot(m.astype(BF16), wout_ref[...]), mpost_ref[...])
    x = _ffn_block(x, fpre_ref, wg_ref, wu_ref, wd_ref, fpost_ref, f_chunk)
    e = jax.nn.sigmoid(_dot(x.astype(BF16), pg_ref[...])) * _dot(p_ref[...].astype(BF16), pp_ref[...])
    y_ref[...] = x + _rms(e, ppost_ref[...])


def _back(x, hg, o, ga, gb, p, w, *, tm):
    n = x.shape[0]
    row = lambda d: pl.BlockSpec((tm, d), lambda i: (i, 0))
    weights = [w["w_branch_rnn"], w["w_branch_attn"], w["w_out"], w["mix_post"], *w["ffn2"],
               w["ple_gate"], w["ple_proj"], w["ple_post"]]
    return pl.pallas_call(
        functools.partial(_back_kernel, f_chunk=D_FF // 2),
        grid=(n // tm,),
        in_specs=[row(D_MODEL), row(D_RNN), row(D_MODEL), row(D_MODEL), row(D_MODEL), row(D_PLE)]
                 + [_resident(a.shape) for a in weights],
        out_specs=row(D_MODEL),
        out_shape=jax.ShapeDtypeStruct((n, D_MODEL), F32),
        compiler_params=_params("parallel"),
        name="back",
    )(x, hg, o, ga, gb, p, *weights)


def _rot_cols(w):
    half = QK_ROPE // 2
    return jnp.concatenate([-w[..., half:], w[..., :half]], axis=-1)


def _pad_axis(a, axis, before, after):
    pads = [(0, 0)] * a.ndim
    pads[axis] = (before, after)
    return jnp.pad(a, pads)


def _prep_layer(i, ffn1_pre, ffn1_w_gate, ffn1_w_up, ffn1_w_down, ffn1_post, mix_pre, w_in, conv_w, conv_b,
                lru_w_a, lru_b_a, lru_w_i, lru_b_i, lru_lambda, w_branch_rnn, q_norm, w_uq, w_qr, kv_norm, w_uk,
                w_uv, w_branch_attn, w_out, mix_post, ffn2_pre, ffn2_w_gate, ffn2_w_up, ffn2_w_down, ffn2_post,
                ple_gate, ple_proj, ple_post):
    vec = lambda a: a[i].reshape(1, -1)
    w_in_i = w_in[i]
    w_kr = w_in_i[:, O_KV:O_KR]
    pad_r = HEAD_PAD - QK_ROPE - QK_NOPE
    w_qr_i, w_uq_i = w_qr[i], w_uq[i]
    same_block = (jnp.arange(D_RNN)[:, None] // RNN_BLOCK) == (jnp.arange(D_RNN)[None, :] // RNN_BLOCK)
    blockdiag = lambda wb: jnp.where(same_block, jnp.tile(wb.reshape(D_RNN, RNN_BLOCK), (1, RNN_BLOCKS)), 0.0)
    return {
        "ffn1": (vec(ffn1_pre), ffn1_w_gate[i].astype(BF16), ffn1_w_up[i].astype(BF16),
                 ffn1_w_down[i].astype(BF16), vec(ffn1_post)),
        "ffn2": (vec(ffn2_pre), ffn2_w_gate[i].astype(BF16), ffn2_w_up[i].astype(BF16),
                 ffn2_w_down[i].astype(BF16), vec(ffn2_post)),
        "mix_pre": vec(mix_pre),
        "w_main": jnp.concatenate([w_in_i[:, :O_KV], w_in_i[:, O_KR:]], axis=1).astype(BF16),
        "w_kr2": jnp.concatenate([_pad_axis(w_kr, 1, 0, HEAD_PAD - QK_ROPE),
                                  _pad_axis(_rot_cols(w_kr), 1, 0, HEAD_PAD - QK_ROPE)], axis=1).astype(BF16),
        "q_norm": vec(q_norm), "kv_norm": vec(kv_norm),
        "w_qrt": jnp.transpose(w_qr_i, (1, 2, 0)).reshape(N_HEADS * QK_ROPE, Q_LORA).astype(BF16),
        "w_qrt_rot": jnp.transpose(_rot_cols(w_qr_i), (1, 2, 0)).reshape(N_HEADS * QK_ROPE, Q_LORA).astype(BF16),
        "w_uqt": jnp.transpose(w_uq_i, (1, 2, 0)).reshape(N_HEADS * QK_NOPE, Q_LORA).astype(BF16),
        "w_uk_pad": _pad_axis(w_uk[i], 2, QK_ROPE, pad_r).reshape(KV_LORA, N_HEADS * HEAD_PAD).astype(BF16),
        "w_uvt": jnp.transpose(w_uv[i], (1, 2, 0)).reshape(N_HEADS * V_HEAD, KV_LORA).astype(BF16),
        "w_uv_pair": jnp.transpose(
            jnp.where((jnp.arange(N_HEADS) % 2 == 1)[None, :, None], _pad_axis(w_uv[i], 2, V_HEAD, 0),
                      _pad_axis(w_uv[i], 2, 0, V_HEAD)), (1, 0, 2)).astype(BF16),
        "w_uq": w_uq_i.reshape(Q_LORA, N_HEADS * QK_NOPE).astype(BF16),
        "w_qr": w_qr_i.reshape(Q_LORA, N_HEADS * QK_ROPE).astype(BF16),
        "w_qr_rot": _rot_cols(w_qr_i).reshape(Q_LORA, N_HEADS * QK_ROPE).astype(BF16),
        "w_ukt3": jnp.transpose(w_uk[i], (1, 2, 0)).astype(BF16),
        "conv_w": conv_w[i], "conv_b": vec(conv_b),
        "w_lru": jnp.concatenate([blockdiag(lru_w_a[i]), blockdiag(lru_w_i[i])], axis=1).astype(BF16),
        "b_lru": jnp.concatenate([lru_b_a[i].reshape(1, -1), lru_b_i[i].reshape(1, -1)], axis=1),
        "lam": vec(lru_lambda),
        "w_branch_rnn": w_branch_rnn[i].astype(BF16),
        "w_branch_attn": w_branch_attn[i].astype(BF16),
        "w_out": w_out[i].astype(BF16), "mix_post": vec(mix_post),
        "ple_gate": ple_gate[i].astype(BF16), "ple_proj": ple_proj[i].astype(BF16), "ple_post": vec(ple_post),
    }


def _rope_tables(pos, *, prompt):
    half = QK_ROPE // 2
    freqs = ROPE_THETA ** (-jnp.arange(half, dtype=F32) / half)
    ang = pos.astype(F32)[:, None] * freqs[None, :]
    cos32 = jnp.tile(jnp.cos(ang), (1, 2))
    sin32 = jnp.tile(jnp.sin(ang), (1, 2))
    n = pos.shape[0]
    tabs = {"cos_k": _pad_axis(cos32, 1, 0, HEAD_PAD - QK_ROPE), "sin_k": _pad_axis(sin32, 1, 0, HEAD_PAD - QK_ROPE)}
    if prompt:
        tabs["cos_qt"] = Q_SCALE * cos32.T
        tabs["sin_qt"] = Q_SCALE * sin32.T
    else:
        tabs["cos512"] = Q_SCALE * jnp.tile(cos32, (1, N_HEADS))
        tabs["sin512"] = Q_SCALE * jnp.tile(sin32, (1, N_HEADS))
    return tabs


def _tile(n, pref):
    return pref if n % pref == 0 else n


def _layer_prompt(x, p, w):
    b, s, _ = x.shape
    n = b * s
    tabs = _rope_tables(jnp.arange(s, dtype=jnp.int32), prompt=True)
    x1, xr, gy, ckv, krope, ga, gb, qt, k, vt = _proj(x, w, tabs, prompt=True, tm=_tile(s, 256))
    x1 = x1.reshape(n, D_MODEL)
    hg, h_last, conv_new = _rglru_prompt(xr, gy, w, ts=_tile(s, 256))
    o = _attn_prompt(qt, k, vt, t=_tile(s, 512))
    y = _back(x1, hg.reshape(n, D_RNN), o.reshape(n, D_MODEL), ga.reshape(n, D_MODEL), gb.reshape(n, D_MODEL),
              p.reshape(n, D_PLE), w, tm=_tile(n, 256))
    return y.reshape(b, s, D_MODEL), (ckv, krope, h_last.reshape(b, D_RNN), conv_new)


def _layer_sample(x, p, h0, conv_buf, pool_ckv, pool_kr, page_table, w):
    n, s, _ = x.shape
    past_len = page_table.shape[1] * PAGE_SIZE
    tabs = _rope_tables(jnp.full((n,), past_len, jnp.int32), prompt=False)
    x1, xr, gy, ckv, krope, ga, gb, qn, qr = _proj(x.reshape(1, n, D_MODEL), w, tabs, prompt=False, tm=n)
    x1 = x1[0]
    hg, h_new, conv_new = _rglru_sample(xr[0], gy[0], jnp.transpose(conv_buf, (1, 0, 2)), h0, w)
    qa = _absorb(qn[0], w["w_ukt3"]).reshape(n, N_HEADS, KV_LORA)
    o_lat = _decode(page_table, qa, qr.reshape(n, N_HEADS, QK_ROPE), ckv.reshape(n, 1, KV_LORA),
                    krope.reshape(n, 1, QK_ROPE), pool_ckv, jnp.transpose(pool_kr, (0, 2, 1)))
    o = _unabsorb(o_lat.reshape(n, N_HEADS * KV_LORA), w["w_uv_pair"])
    y = _back(x1, hg, o, ga[0], gb[0], p.reshape(n, D_PLE), w, tm=n)
    return (y.reshape(n, s, D_MODEL),
            (ckv.reshape(n, s, KV_LORA), krope.reshape(n, s, QK_ROPE), h_new, jnp.transpose(conv_new, (1, 0, 2))))


def kernel(x_prompt, x_sample, p_prompt, p_sample, cache_ckv, cache_krope, state_h, state_conv, page_table,
           ffn1_pre, ffn1_w_gate, ffn1_w_up, ffn1_w_down, ffn1_post, mix_pre, w_in, conv_w, conv_b, lru_w_a,
           lru_b_a, lru_w_i, lru_b_i, lru_lambda, w_branch_rnn, q_norm, w_uq, w_qr, kv_norm, w_uk, w_uv,
           w_branch_attn, w_out, mix_post, ffn2_pre, ffn2_w_gate, ffn2_w_up, ffn2_w_down, ffn2_post, ple_gate,
           ple_proj, ple_post):
    assert x_sample.shape[1] == 1, "the sample group carries one new token per sequence"
    depth = ffn1_pre.shape[0]
    hp, hs = x_prompt, x_sample
    st_p, st_s = [], []
    for i in range(depth):
        w = _prep_layer(i, ffn1_pre, ffn1_w_gate, ffn1_w_up, ffn1_w_down, ffn1_post, mix_pre, w_in, conv_w, conv_b,
                        lru_w_a, lru_b_a, lru_w_i, lru_b_i, lru_lambda, w_branch_rnn, q_norm, w_uq, w_qr, kv_norm,
                        w_uk, w_uv, w_branch_attn, w_out, mix_post, ffn2_pre, ffn2_w_gate, ffn2_w_up, ffn2_w_down,
                        ffn2_post, ple_gate, ple_proj, ple_post)
        hp, sp = _layer_prompt(hp, p_prompt[i], w)
        hs, ss = _layer_sample(hs, p_sample[i], state_h[i], state_conv[i], cache_ckv[i], cache_krope[i],
                               page_table, w)
        st_p.append(sp)
        st_s.append(ss)
    stack = lambda sts, k: jnp.stack([s[k] for s in sts])
    return (hp, hs, stack(st_p, 0), stack(st_p, 1), stack(st_p, 2), stack(st_p, 3),
            stack(st_s, 0), stack(st_s, 1), stack(st_s, 2), stack(st_s, 3))
```

```python
import functools

import jax
import jax.numpy as jnp
from jax import lax
from jax.experimental import pallas as pl
from jax.experimental.pallas import tpu as pltpu

D_MODEL = 1024
D_RNN = 1280
RNN_BLOCKS = 16
RNN_BLOCK = D_RNN // RNN_BLOCKS
CONV_W = 4
LRU_C = 8.0
N_HEADS = 16
QK_NOPE = 64
QK_ROPE = 32
V_HEAD = 64
Q_LORA = 384
KV_LORA = 256
ROPE_THETA = 10000.0
SM_SCALE = (QK_NOPE + QK_ROPE) ** -0.5
Q_SCALE = SM_SCALE * 1.4426950408889634
D_FF = 2816
D_PLE = 256
EPS = 1e-6
PAGE_SIZE = 128
O_XR = D_RNN
O_YR = O_XR + D_RNN
O_Q = O_YR + Q_LORA
O_KV = O_Q + KV_LORA
O_KR = O_KV + QK_ROPE

LANES = 128
HEAD_PAD = LANES
VMEM_LIMIT_BYTES = 56 * 1024 * 1024

F32 = jnp.float32
BF16 = jnp.bfloat16


def _rms(x, g):
    return x * lax.rsqrt(jnp.mean(x * x, axis=-1, keepdims=True) + EPS) * g


def _dot(a, b):
    return jnp.dot(a, b, preferred_element_type=F32)


def _dot_nt(a, b):
    return lax.dot_general(a, b, (((1,), (1,)), ((), ())), preferred_element_type=F32)


def _resident(shape):
    nd = len(shape)
    return pl.BlockSpec(shape, lambda *_: (0,) * nd, pipeline_mode=pl.Buffered(1))


def _params(*sem):
    return pltpu.CompilerParams(dimension_semantics=sem, vmem_limit_bytes=VMEM_LIMIT_BYTES)


def _ffn_block(x, pre_ref, wg_ref, wu_ref, wd_ref, post_ref, f_chunk):
    u = _rms(x, pre_ref[...]).astype(BF16)
    acc = jnp.zeros(x.shape, F32)
    for c in range(D_FF // f_chunk):
        sl = slice(c * f_chunk, (c + 1) * f_chunk)
        g = _dot(u, wg_ref[:, sl])
        h = (g * jax.nn.sigmoid(g)) * _dot(u, wu_ref[:, sl])
        acc = acc + _dot(h.astype(BF16), wd_ref[sl, :])
    return x + 0.5 * _rms(acc, post_ref[...])


def _ffn_kernel(x_ref, pre_ref, wg_ref, wu_ref, wd_ref, post_ref, o_ref, *, f_chunk):
    o_ref[...] = _ffn_block(x_ref[...], pre_ref, wg_ref, wu_ref, wd_ref, post_ref, f_chunk)


def _ffn(x, pre, wg, wu, wd, post, *, tm):
    n = x.shape[0]
    return pl.pallas_call(
        functools.partial(_ffn_kernel, f_chunk=D_FF // 2),
        grid=(n // tm,),
        in_specs=[pl.BlockSpec((tm, D_MODEL), lambda i: (i, 0)),
                  _resident((1, D_MODEL)), _resident((D_MODEL, D_FF)), _resident((D_MODEL, D_FF)),
                  _resident((D_FF, D_MODEL)), _resident((1, D_MODEL))],
        out_specs=pl.BlockSpec((tm, D_MODEL), lambda i: (i, 0)),
        out_shape=jax.ShapeDtypeStruct((n, D_MODEL), F32),
        compiler_params=_params("parallel"),
        name="ffn",
    )(x, pre, wg, wu, wd, post)


N_MAIN = 2 * D_RNN + Q_LORA + KV_LORA + 2 * D_MODEL
M_YR = D_RNN
M_CQ = 2 * D_RNN
M_KV = M_CQ + Q_LORA
M_GA = M_KV + KV_LORA
M_GB = M_GA + D_MODEL


def _proj_kernel(*refs, prompt):
    (x_ref, pre_ref, wmain_ref, qn_ref, kvn_ref, wkr_ref, cosk_ref, sink_ref), refs = refs[:8], refs[8:]
    if prompt:
        (wqrt_ref, wqrtrot_ref, wuqt_ref, cosqt_ref, sinqt_ref, wuk_ref, wuvt_ref), refs = refs[:7], refs[7:]
    else:
        (wuq_ref, wqr_ref, wqrrot_ref, cos512_ref, sin512_ref), refs = refs[:5], refs[5:]
    (xr_ref, gy_ref, ckv_ref, kr_ref, ga_ref, gb_ref), refs = refs[:6], refs[6:]

    u = _rms(x_ref[0], pre_ref[...]).astype(BF16)
    xr_ref[0] = _dot(u, wmain_ref[:, 0:M_YR])
    gy_ref[0] = jax.nn.gelu(_dot(u, wmain_ref[:, M_YR:M_CQ])).astype(BF16)
    ga_ref[0] = jax.nn.sigmoid(_dot(u, wmain_ref[:, M_GA:M_GB])).astype(BF16)
    gb_ref[0] = jax.nn.sigmoid(_dot(u, wmain_ref[:, M_GB:N_MAIN])).astype(BF16)
    cq = _rms(_dot(u, wmain_ref[:, M_CQ:M_KV]), qn_ref[...]).astype(BF16)
    ckv = _rms(_dot(u, wmain_ref[:, M_KV:M_GA]), kvn_ref[...])
    ckv_ref[0] = ckv
    ckv_b = ckv.astype(BF16)
    kr2 = _dot(u, wkr_ref[...])
    kr = kr2[:, :HEAD_PAD] * cosk_ref[...] + kr2[:, HEAD_PAD:] * sink_ref[...]
    kr_ref[0] = kr[:, :QK_ROPE]

    if prompt:
        qt_ref, k_ref, vt_ref = refs
        tm = cq.shape[0]
        k_nope = _dot(ckv_b, wuk_ref[...])
        qr_t = _dot_nt(wqrt_ref[...], cq)
        qrot_t = _dot_nt(wqrtrot_ref[...], cq)
        qn_t = _dot_nt(wuqt_ref[...], cq) * Q_SCALE
        v_t = _dot_nt(wuvt_ref[...], ckv_b)
        cos_t = cosqt_ref[...]
        sin_t = sinqt_ref[...]
        q_pad = jnp.zeros((HEAD_PAD - QK_ROPE - QK_NOPE, tm), BF16)
        ones_row = (lax.broadcasted_iota(jnp.int32, (HEAD_PAD - V_HEAD, tm), 0) == 0).astype(BF16)
        for h in range(N_HEADS):
            r0 = h * HEAD_PAD
            k_ref[0, h] = (k_nope[:, r0:r0 + HEAD_PAD] + kr).astype(BF16)
            rope = slice(h * QK_ROPE, (h + 1) * QK_ROPE)
            qt_ref[0, r0:r0 + QK_ROPE, :] = (qr_t[rope] * cos_t + qrot_t[rope] * sin_t).astype(BF16)
            qt_ref[0, r0 + QK_ROPE:r0 + QK_ROPE + QK_NOPE, :] = qn_t[h * QK_NOPE:(h + 1) * QK_NOPE].astype(BF16)
            qt_ref[0, r0 + QK_ROPE + QK_NOPE:r0 + HEAD_PAD, :] = q_pad
            vt_ref[0, r0:r0 + V_HEAD, :] = v_t[h * V_HEAD:(h + 1) * V_HEAD].astype(BF16)
            vt_ref[0, r0 + V_HEAD:r0 + HEAD_PAD, :] = ones_row
    else:
        qn_out_ref, qr_out_ref = refs
        qn_out_ref[0] = (_dot(cq, wuq_ref[...]) * Q_SCALE).astype(BF16)
        qr_out_ref[0] = (_dot(cq, wqr_ref[...]) * cos512_ref[...]
                         + _dot(cq, wqrrot_ref[...]) * sin512_ref[...]).astype(BF16)


def _proj(x, w, tabs, *, prompt, tm):
    b, s, _ = x.shape
    row = lambda d: pl.BlockSpec((1, tm, d), lambda bi, i: (bi, i, 0))
    tab = lambda d: pl.BlockSpec((tm, d), lambda bi, i: (i, 0))
    tab_t = pl.BlockSpec((QK_ROPE, tm), lambda bi, i: (0, i))
    in_specs = [row(D_MODEL), _resident((1, D_MODEL)), _resident((D_MODEL, N_MAIN)), _resident((1, Q_LORA)),
                _resident((1, KV_LORA)), _resident((D_MODEL, 2 * HEAD_PAD)), tab(HEAD_PAD), tab(HEAD_PAD)]
    args = [x, w["mix_pre"], w["w_main"], w["q_norm"], w["kv_norm"], w["w_kr2"], tabs["cos_k"], tabs["sin_k"]]
    out_specs = [row(D_RNN), row(D_RNN), row(KV_LORA), row(QK_ROPE), row(D_MODEL), row(D_MODEL)]
    out_shape = [jax.ShapeDtypeStruct((b, s, D_RNN), F32), jax.ShapeDtypeStruct((b, s, D_RNN), BF16),
                 jax.ShapeDtypeStruct((b, s, KV_LORA), F32), jax.ShapeDtypeStruct((b, s, QK_ROPE), F32),
                 jax.ShapeDtypeStruct((b, s, D_MODEL), BF16), jax.ShapeDtypeStruct((b, s, D_MODEL), BF16)]
    hp = N_HEADS * HEAD_PAD
    if prompt:
        in_specs += [_resident((N_HEADS * QK_ROPE, Q_LORA)), _resident((N_HEADS * QK_ROPE, Q_LORA)),
                     _resident((N_HEADS * QK_NOPE, Q_LORA)), tab_t, tab_t,
                     _resident((KV_LORA, hp)), _resident((N_HEADS * V_HEAD, KV_LORA))]
        args += [w["w_qrt"], w["w_qrt_rot"], w["w_uqt"], tabs["cos_qt"], tabs["sin_qt"], w["w_uk_pad"], w["w_uvt"]]
        lanes_major = pl.BlockSpec((1, hp, tm), lambda bi, i: (bi, 0, i))
        out_specs += [lanes_major, pl.BlockSpec((1, N_HEADS, tm, HEAD_PAD), lambda bi, i: (bi, 0, i, 0)), lanes_major]
        out_shape += [jax.ShapeDtypeStruct((b, hp, s), BF16),
                      jax.ShapeDtypeStruct((b, N_HEADS, s, HEAD_PAD), BF16),
                      jax.ShapeDtypeStruct((b, hp, s), BF16)]
    else:
        nn, nr = N_HEADS * QK_NOPE, N_HEADS * QK_ROPE
        in_specs += [_resident((Q_LORA, nn)), _resident((Q_LORA, nr)), _resident((Q_LORA, nr)), tab(nr), tab(nr)]
        args += [w["w_uq"], w["w_qr"], w["w_qr_rot"], tabs["cos512"], tabs["sin512"]]
        out_specs += [row(nn), row(nr)]
        out_shape += [jax.ShapeDtypeStruct((b, s, nn), BF16), jax.ShapeDtypeStruct((b, s, nr), BF16)]
    return pl.pallas_call(
        functools.partial(_proj_kernel, prompt=prompt),
        grid=(b, s // tm),
        in_specs=in_specs, out_specs=out_specs, out_shape=out_shape,
        compiler_params=_params("parallel", "parallel"),
        name="proj_prompt" if prompt else "proj_sample",
    )(*args)


def _softplus(x):
    return jnp.maximum(x, 0.0) + jnp.log1p(jnp.exp(-jnp.abs(x)))


def _sigmoid(z):
    return 0.5 * jnp.tanh(0.5 * z) + 0.5


def _lru_coeffs(xc, wbd_ref, bab_ref, lam_ref):
    z = _dot(xc.astype(BF16), wbd_ref[...]) + bab_ref[...]
    r = _sigmoid(z[:, :D_RNN])
    gi = _sigmoid(z[:, D_RNN:])
    log_a = -LRU_C * r * _softplus(-lam_ref[...])
    a = jnp.exp(log_a)
    th = jnp.tanh(log_a)
    mult = jnp.sqrt(-2.0 * th / (1.0 - th))
    return a, mult * (gi * xc)


SUBLANES = 8


def _shift_rows(x, tail, k):
    rolled = pltpu.roll(x, k, axis=0)
    head_rows = lax.broadcasted_iota(jnp.int32, tail.shape, 0)
    first = jnp.where(head_rows < k, pltpu.roll(tail, k, axis=0), rolled[:SUBLANES])
    return jnp.concatenate([first, rolled[SUBLANES:]], axis=0)


def _rglru_prompt_kernel(xr_ref, gy_ref, cw_ref, cb_ref, wbd_ref, bab_ref, lam_ref,
                         hg_ref, hlast_ref, convnew_ref, tail_s, a_s, b_s, h_s, hcar, *, ts):
    t = pl.program_id(0)
    nb = xr_ref.shape[0]

    @pl.when(t == 0)
    def _():
        tail_s[...] = jnp.zeros(tail_s.shape, F32)
        hcar[...] = jnp.zeros(hcar.shape, F32)

    for bi in range(nb):
        x = xr_ref[bi]
        tail = tail_s[bi]
        xc = cb_ref[...] + _shift_rows(x, tail, 3) * cw_ref[0:1, :]
        xc = xc + _shift_rows(x, tail, 2) * cw_ref[1:2, :]
        xc = xc + _shift_rows(x, tail, 1) * cw_ref[2:3, :]
        xc = xc + x * cw_ref[3:4, :]
        tail_s[bi] = x[ts - SUBLANES:, :]
        a, b = _lru_coeffs(xc, wbd_ref, bab_ref, lam_ref)
        a_s[bi] = a
        b_s[bi] = b

    def step(i, hs):
        out = []
        for bi in range(nb):
            h = a_s[bi, pl.ds(i, 1), :] * hs[bi] + b_s[bi, pl.ds(i, 1), :]
            h_s[bi, pl.ds(i, 1), :] = h
            out.append(h)
        return tuple(out)

    hs = lax.fori_loop(0, ts, step, tuple(hcar[bi] for bi in range(nb)), unroll=8)
    for bi in range(nb):
        hcar[bi] = hs[bi]
        hg_ref[bi] = (h_s[bi] * gy_ref[bi].astype(F32)).astype(BF16)

    @pl.when(t == pl.num_programs(0) - 1)
    def _():
        for bi in range(nb):
            hlast_ref[bi] = hs[bi]
            convnew_ref[bi] = xr_ref[bi, ts - (CONV_W - 1):, :]


def _rglru_prompt(xr, gy, w, *, ts):
    b, s, _ = xr.shape
    rows = pl.BlockSpec((b, ts, D_RNN), lambda t: (0, t, 0))
    whole = lambda d: pl.BlockSpec((b, d, D_RNN), lambda t: (0, 0, 0))
    return pl.pallas_call(
        functools.partial(_rglru_prompt_kernel, ts=ts),
        grid=(s // ts,),
        in_specs=[rows, rows, _resident((CONV_W, D_RNN)), _resident((1, D_RNN)), _resident((D_RNN, 2 * D_RNN)),
                  _resident((1, 2 * D_RNN)), _resident((1, D_RNN))],
        out_specs=[rows, whole(1), whole(CONV_W - 1)],
        out_shape=[jax.ShapeDtypeStruct((b, s, D_RNN), BF16), jax.ShapeDtypeStruct((b, 1, D_RNN), F32),
                   jax.ShapeDtypeStruct((b, CONV_W - 1, D_RNN), F32)],
        scratch_shapes=[pltpu.VMEM((b, SUBLANES, D_RNN), F32), pltpu.VMEM((b, ts, D_RNN), F32),
                        pltpu.VMEM((b, ts, D_RNN), F32), pltpu.VMEM((b, ts, D_RNN), F32),
                        pltpu.VMEM((b, 1, D_RNN), F32)],
        compiler_params=_params("arbitrary"),
        name="rglru_prompt",
    )(xr, gy, w["conv_w"], w["conv_b"], w["w_lru"], w["b_lru"], w["lam"])


def _rglru_sample_kernel(xr_ref, gy_ref, sc_ref, h0_ref, cw_ref, cb_ref, wbd_ref, bab_ref, lam_ref,
                         hg_ref, hnew_ref, convnew_ref):
    x = xr_ref[...]
    xc = cb_ref[...] + sc_ref[0] * cw_ref[0:1, :]
    xc = xc + sc_ref[1] * cw_ref[1:2, :]
    xc = xc + sc_ref[2] * cw_ref[2:3, :]
    xc = xc + x * cw_ref[3:4, :]
    a, b = _lru_coeffs(xc, wbd_ref, bab_ref, lam_ref)
    h = a * h0_ref[...] + b
    hnew_ref[...] = h
    hg_ref[...] = (h * gy_ref[...].astype(F32)).astype(BF16)
    convnew_ref[0] = sc_ref[1]
    convnew_ref[1] = sc_ref[2]
    convnew_ref[2] = x


def _rglru_sample(xr, gy, sc, h0, w):
    n = xr.shape[0]
    return pl.pallas_call(
        _rglru_sample_kernel,
        out_shape=[jax.ShapeDtypeStruct((n, D_RNN), BF16), jax.ShapeDtypeStruct((n, D_RNN), F32),
                   jax.ShapeDtypeStruct((CONV_W - 1, n, D_RNN), F32)],
        compiler_params=pltpu.CompilerParams(vmem_limit_bytes=VMEM_LIMIT_BYTES),
        name="rglru_sample",
    )(xr, gy, sc, h0, w["conv_w"], w["conv_b"], w["w_lru"], w["b_lru"], w["lam"])


HEADS_PER_STEP = 4


def _attn_prompt_kernel(qt_ref, k_ref, vt_ref, o_ref, m_s, acc_s, s_a, s_b, *, t):
    qq = pl.program_id(2)
    heads = range(HEADS_PER_STEP)
    m_s[...] = jnp.full(m_s.shape, -jnp.inf, F32)
    acc_s[...] = jnp.zeros(acc_s.shape, F32)

    def rows(hh):
        return slice(hh * HEAD_PAD, (hh + 1) * HEAD_PAD)

    def scores(stage, j, hh):
        k0 = pl.multiple_of(j * t, t)
        return _dot(k_ref[0, hh, pl.ds(k0, t), :], qt_ref[0, rows(hh), stage * t:(stage + 1) * t])

    def consume(stage, s, j, hh, masked=False):
        k0 = pl.multiple_of(j * t, t)
        if masked:
            kpos = lax.broadcasted_iota(jnp.int32, (t, t), 0)
            qpos = lax.broadcasted_iota(jnp.int32, (t, t), 1)
            s = jnp.where(kpos <= qpos, s, -jnp.inf)
        m_old = m_s[stage, hh]
        m_new = jnp.maximum(m_old, jnp.max(s, axis=0, keepdims=True))
        p = jnp.exp2(s - m_new).astype(BF16)
        acc_s[stage, hh] = (jnp.exp2(m_old - m_new) * acc_s[stage, hh]
                            + _dot(vt_ref[0, rows(hh), pl.ds(k0, t)], p))
        m_s[stage, hh] = m_new

    def step(stage, j, src, dst, nxt):
        for hh in heads:
            dst[hh] = scores(nxt[0], nxt[1], hh)
            consume(stage, src[hh], j, hh)

    def finish(stage):
        o_t = jnp.concatenate(
            [acc_s[stage, hh][:V_HEAD] * (1.0 / acc_s[stage, hh][V_HEAD:V_HEAD + 1]) for hh in heads], axis=0)
        o_ref[0, stage * t:(stage + 1) * t, :] = o_t.T.astype(BF16)

    for hh in heads:
        s_a[hh] = scores(0, 0, hh)

    def pair0(i, c):
        step(0, 2 * i, s_a, s_b, (0, 2 * i + 1))
        step(0, 2 * i + 1, s_b, s_a, (0, 2 * i + 2))
        return c

    def run_pairs(pair):
        lax.fori_loop(0, qq // 2, lambda i, c: pair(2 * i + 1, pair(2 * i, c)), 0)

        @pl.when(qq % 2 == 1)
        def _():
            pair(qq - 1, 0)

    run_pairs(pair0)
    for hh in heads:
        s_b[hh] = scores(1, 0, hh)
        consume(0, s_a[hh], 2 * qq, hh, masked=True)
    finish(0)

    def pair1(i, c):
        step(1, 2 * i, s_b, s_a, (1, 2 * i + 1))
        step(1, 2 * i + 1, s_a, s_b, (1, 2 * i + 2))
        return c

    run_pairs(pair1)
    step(1, 2 * qq, s_b, s_a, (1, 2 * qq + 1))
    for hh in heads:
        consume(1, s_a[hh], 2 * qq + 1, hh, masked=True)
    finish(1)


def _attn_prompt(qt, k, vt, *, t):
    b, h, s, _ = k.shape
    hps = HEADS_PER_STEP
    assert (hps * V_HEAD) % LANES == 0 and h % hps == 0 and s % (2 * t) == 0
    return pl.pallas_call(
        functools.partial(_attn_prompt_kernel, t=t),
        grid=(b, h // hps, s // (2 * t)),
        in_specs=[pl.BlockSpec((1, hps * HEAD_PAD, 2 * t), lambda bi, hi, i: (bi, hi, i)),
                  pl.BlockSpec((1, hps, s, HEAD_PAD), lambda bi, hi, i: (bi, hi, 0, 0)),
                  pl.BlockSpec((1, hps * HEAD_PAD, s), lambda bi, hi, i: (bi, hi, 0))],
        out_specs=pl.BlockSpec((1, 2 * t, hps * V_HEAD), lambda bi, hi, i: (bi, i, hi)),
        out_shape=jax.ShapeDtypeStruct((b, s, h * V_HEAD), BF16),
        scratch_shapes=[pltpu.VMEM((2, hps, 1, t), F32), pltpu.VMEM((2, hps, HEAD_PAD, t), F32),
                        pltpu.VMEM((hps, t, t), F32), pltpu.VMEM((hps, t, t), F32)],
        compiler_params=_params("parallel", "parallel", "arbitrary"),
        name="attn_prompt",
    )(qt, k, vt)


def _absorb_kernel(qn_ref, wukt_ref, qa_ref):
    qn = qn_ref[...]
    for h in range(N_HEADS):
        qa_ref[:, h * KV_LORA:(h + 1) * KV_LORA] = _dot(
            qn[:, h * QK_NOPE:(h + 1) * QK_NOPE], wukt_ref[h]).astype(BF16)


def _absorb(qn, wukt3):
    n = qn.shape[0]
    return pl.pallas_call(
        _absorb_kernel,
        out_shape=jax.ShapeDtypeStruct((n, N_HEADS * KV_LORA), BF16),
        compiler_params=pltpu.CompilerParams(vmem_limit_bytes=VMEM_LIMIT_BYTES),
        name="absorb",
    )(qn, wukt3)


def _decode_kernel(pt_ref, qa_ref, qr_ref, cn_ref, kn_ref, pool_ckv, pool_krt, o_ref,
                   ckv_buf, krt_buf, page_a, page_b, p_a, p_b, tail_a, tail_b, l_a, l_b, sems, *, n_seq, n_pages):
    b = pl.program_id(0)
    slot = lax.rem(b, 2)
    set_a = (page_a, p_a, tail_a, l_a)
    set_b = (page_b, p_b, tail_b, l_b)

    def page_copies(page, sl, p):
        return (pltpu.make_async_copy(pool_ckv.at[page], ckv_buf.at[sl, p], sems.at[sl, 0]),
                pltpu.make_async_copy(pool_krt.at[page], krt_buf.at[sl, p], sems.at[sl, 1]))

    def start_row(row, sl):
        def body(p, c):
            for cp in page_copies(pt_ref[row * n_pages + p], sl, p):
                cp.start()
            return c
        lax.fori_loop(0, n_pages, body, 0)

    @pl.when(b == 0)
    def _():
        start_row(0, 0)
        page_b[...] = jnp.zeros(page_b.shape, BF16)
        p_b[...] = jnp.zeros(p_b.shape, BF16)
        tail_b[...] = jnp.zeros(tail_b.shape, F32)
        l_b[...] = jnp.ones(l_b.shape, F32)

    @pl.when(b + 1 < n_seq)
    def _():
        start_row(b + 1, 1 - slot)

    @pl.when(b < n_seq)
    def _():
        for p in range(n_pages):
            for cp in page_copies(0, slot, p):
                cp.wait()

    def score(cur):
        page_c, p_c, tail_c, l_c = cur
        qa = qa_ref[0]
        qr = qr_ref[0]
        s_pairs = []
        for p in range(0, n_pages, 2):
            pair = jnp.concatenate([ckv_buf[slot, p], ckv_buf[slot, p + 1]], axis=0).astype(BF16)
            page_c[p] = pair[:PAGE_SIZE]
            page_c[p + 1] = pair[PAGE_SIZE:]
            kr_pair = jnp.concatenate([krt_buf[slot, p], krt_buf[slot, p + 1]], axis=1).astype(BF16)
            s_pairs.append(_dot_nt(qa, pair) + _dot(qr, kr_pair))
        s = jnp.concatenate(s_pairs, axis=1)
        cn = cn_ref[0].astype(BF16).astype(F32)
        kn = kn_ref[0].astype(BF16).astype(F32)
        s_new = (jnp.sum(qa.astype(F32) * cn, axis=-1, keepdims=True)
                 + jnp.sum(qr.astype(F32) * kn, axis=-1, keepdims=True))
        m = jnp.maximum(jnp.max(s, axis=-1, keepdims=True), s_new)
        p_past = jnp.exp2(s - m)
        p_new = jnp.exp2(s_new - m)
        l_c[...] = jnp.sum(p_past, axis=-1, keepdims=True) + p_new
        tail_c[...] = p_new * cn
        p_c[...] = p_past.astype(BF16)

    def attend(prv):
        page_p, p_p, tail_p, l_p = prv
        acc = tail_p[...]
        for p in range(n_pages):
            acc = acc + _dot(p_p[:, p * PAGE_SIZE:(p + 1) * PAGE_SIZE], page_p[p])
        o_ref[0] = (acc / l_p[...]).astype(BF16)

    @pl.when(jnp.logical_and(b < n_seq, slot == 0))
    def _():
        score(set_a)
        attend(set_b)

    @pl.when(jnp.logical_and(b < n_seq, slot == 1))
    def _():
        score(set_b)
        attend(set_a)

    @pl.when(b == n_seq)
    def _():
        attend(set_a if (n_seq - 1) % 2 == 0 else set_b)


def _decode(page_table, qa, qr, ckv_new, kr_new, pool_ckv, pool_krt):
    n, n_pages = page_table.shape
    per_row = lambda d0, d1: pl.BlockSpec((1, d0, d1), lambda bi, pt: (jnp.minimum(bi, n - 1), 0, 0))
    hbm = pl.BlockSpec(memory_space=pl.ANY)
    page_set = [pltpu.VMEM((n_pages, PAGE_SIZE, KV_LORA), BF16)] * 2
    grid_spec = pltpu.PrefetchScalarGridSpec(
        num_scalar_prefetch=1,
        grid=(n + 1,),
        in_specs=[per_row(N_HEADS, KV_LORA), per_row(N_HEADS, QK_ROPE), per_row(1, KV_LORA), per_row(1, QK_ROPE),
                  hbm, hbm],
        out_specs=pl.BlockSpec((1, N_HEADS, KV_LORA), lambda bi, pt: (jnp.maximum(bi - 1, 0), 0, 0)),
        scratch_shapes=[pltpu.VMEM((2, n_pages, PAGE_SIZE, KV_LORA), F32),
                        pltpu.VMEM((2, n_pages, QK_ROPE, PAGE_SIZE), F32),
                        *page_set,
                        *([pltpu.VMEM((N_HEADS, n_pages * PAGE_SIZE), BF16)] * 2),
                        *([pltpu.VMEM((N_HEADS, KV_LORA), F32)] * 2),
                        *([pltpu.VMEM((N_HEADS, 1), F32)] * 2),
                        pltpu.SemaphoreType.DMA((2, 2))],
    )
    return pl.pallas_call(
        functools.partial(_decode_kernel, n_seq=n, n_pages=n_pages),
        grid_spec=grid_spec,
        out_shape=jax.ShapeDtypeStruct((n, N_HEADS, KV_LORA), BF16),
        compiler_params=_params("arbitrary"),
        name="decode",
    )(page_table.reshape(-1), qa, qr, ckv_new, kr_new, pool_ckv, pool_krt)


def _unabsorb_kernel(ol_ref, wuv_ref, o_ref):
    for h2 in range(N_HEADS // 2):
        pair = jnp.zeros((ol_ref.shape[0], LANES), F32)
        for h in (2 * h2, 2 * h2 + 1):
            pair = pair + _dot(ol_ref[:, h * KV_LORA:(h + 1) * KV_LORA], wuv_ref[h])
        o_ref[:, h2 * LANES:(h2 + 1) * LANES] = pair.astype(BF16)


def _unabsorb(o_lat, w_uv_pair):
    n = o_lat.shape[0]
    return pl.pallas_call(
        _unabsorb_kernel,
        out_shape=jax.ShapeDtypeStruct((n, N_HEADS * V_HEAD), BF16),
        compiler_params=pltpu.CompilerParams(vmem_limit_bytes=VMEM_LIMIT_BYTES),
        name="unabsorb",
    )(o_lat, w_uv_pair)


def _back_kernel(x_ref, hg_ref, o_ref, ga_ref, gb_ref, p_ref, wrnn_ref, wattn_ref, wout_ref, mpost_ref,
                 fpre_ref, wg_ref, wu_ref, wd_ref, fpost_ref, pg_ref, pp_ref, ppost_ref, y_ref, *, f_chunk):
    y_a = _dot(hg_ref[...], wrnn_ref[...])
    y_b = _dot(o_ref[...], wattn_ref[...])
    m = ga_ref[...].astype(F32) * y_a + gb_ref[...].astype(F32) * y_b
    x = x_ref[...] + _rms(_dot(m.astype(BF16), wout_ref[...]), mpost_ref[...])
    x = _ffn_block(x, fpre_ref, wg_ref, wu_ref, wd_ref, fpost_ref, f_chunk)
    e = jax.nn.sigmoid(_dot(x.astype(BF16), pg_ref[...])) * _dot(p_ref[...].astype(BF16), pp_ref[...])
    y_ref[...] = x + _rms(e, ppost_ref[...])


def _back(x, hg, o, ga, gb, p, w, *, tm):
    n = x.shape[0]
    row = lambda d: pl.BlockSpec((tm, d), lambda i: (i, 0))
    weights = [w["w_branch_rnn"], w["w_branch_attn"], w["w_out"], w["mix_post"], *w["ffn2"],
               w["ple_gate"], w["ple_proj"], w["ple_post"]]
    return pl.pallas_call(
        functools.partial(_back_kernel, f_chunk=D_FF // 2),
        grid=(n // tm,),
        in_specs=[row(D_MODEL), row(D_RNN), row(D_MODEL), row(D_MODEL), row(D_MODEL), row(D_PLE)]
                 + [_resident(a.shape) for a in weights],
        out_specs=row(D_MODEL),
        out_shape=jax.ShapeDtypeStruct((n, D_MODEL), F32),
        compiler_params=_params("parallel"),
        name="back",
    )(x, hg, o, ga, gb, p, *weights)


def _rot_cols(w):
    half = QK_ROPE // 2
    return jnp.concatenate([-w[..., half:], w[..., :half]], axis=-1)


def _pad_axis(a, axis, before, after):
    pads = [(0, 0)] * a.ndim
    pads[axis] = (before, after)
    return jnp.pad(a, pads)


def _prep_layer(i, ffn1_pre, ffn1_w_gate, ffn1_w_up, ffn1_w_down, ffn1_post, mix_pre, w_in, conv_w, conv_b,
                lru_w_a, lru_b_a, lru_w_i, lru_b_i, lru_lambda, w_branch_rnn, q_norm, w_uq, w_qr, kv_norm, w_uk,
                w_uv, w_branch_attn, w_out, mix_post, ffn2_pre, ffn2_w_gate, ffn2_w_up, ffn2_w_down, ffn2_post,
                ple_gate, ple_proj, ple_post):
    vec = lambda a: a[i].reshape(1, -1)
    w_in_i = w_in[i]
    w_kr = w_in_i[:, O_KV:O_KR]
    pad_r = HEAD_PAD - QK_ROPE - QK_NOPE
    w_qr_i, w_uq_i = w_qr[i], w_uq[i]
    same_block = (jnp.arange(D_RNN)[:, None] // RNN_BLOCK) == (jnp.arange(D_RNN)[None, :] // RNN_BLOCK)
    blockdiag = lambda wb: jnp.where(same_block, jnp.tile(wb.reshape(D_RNN, RNN_BLOCK), (1, RNN_BLOCKS)), 0.0)
    return {
        "ffn1": (vec(ffn1_pre), ffn1_w_gate[i].astype(BF16), ffn1_w_up[i].astype(BF16),
                 ffn1_w_down[i].astype(BF16), vec(ffn1_post)),
        "ffn2": (vec(ffn2_pre), ffn2_w_gate[i].astype(BF16), ffn2_w_up[i].astype(BF16),
                 ffn2_w_down[i].astype(BF16), vec(ffn2_post)),
        "mix_pre": vec(mix_pre),
        "w_main": jnp.concatenate([w_in_i[:, :O_KV], w_in_i[:, O_KR:]], axis=1).astype(BF16),
        "w_kr2": jnp.concatenate([_pad_axis(w_kr, 1, 0, HEAD_PAD - QK_ROPE),
                                  _pad_axis(_rot_cols(w_kr), 1, 0, HEAD_PAD - QK_ROPE)], axis=1).astype(BF16),
        "q_norm": vec(q_norm), "kv_norm": vec(kv_norm),
        "w_qrt": jnp.transpose(w_qr_i, (1, 2, 0)).reshape(N_HEADS * QK_ROPE, Q_LORA).astype(BF16),
        "w_qrt_rot": jnp.transpose(_rot_cols(w_qr_i), (1, 2, 0)).reshape(N_HEADS * QK_ROPE, Q_LORA).astype(BF16),
        "w_uqt": jnp.transpose(w_uq_i, (1, 2, 0)).reshape(N_HEADS * QK_NOPE, Q_LORA).astype(BF16),
        "w_uk_pad": _pad_axis(w_uk[i], 2, QK_ROPE, pad_r).reshape(KV_LORA, N_HEADS * HEAD_PAD).astype(BF16),
        "w_uvt": jnp.transpose(w_uv[i], (1, 2, 0)).reshape(N_HEADS * V_HEAD, KV_LORA).astype(BF16),
        "w_uv_pair": jnp.transpose(
            jnp.where((jnp.arange(N_HEADS) % 2 == 1)[None, :, None], _pad_axis(w_uv[i], 2, V_HEAD, 0),
                      _pad_axis(w_uv[i], 2, 0, V_HEAD)), (1, 0, 2)).astype(BF16),
        "w_uq": w_uq_i.reshape(Q_LORA, N_HEADS * QK_NOPE).astype(BF16),
        "w_qr": w_qr_i.reshape(Q_LORA, N_HEADS * QK_ROPE).astype(BF16),
        "w_qr_rot": _rot_cols(w_qr_i).reshape(Q_LORA, N_HEADS * QK_ROPE).astype(BF16),
        "w_ukt3": jnp.transpose(w_uk[i], (1, 2, 0)).astype(BF16),
        "conv_w": conv_w[i], "conv_b": vec(conv_b),
        "w_lru": jnp.concatenate([blockdiag(lru_w_a[i]), blockdiag(lru_w_i[i])], axis=1).astype(BF16),
        "b_lru": jnp.concatenate([lru_b_a[i].reshape(1, -1), lru_b_i[i].reshape(1, -1)], axis=1),
        "lam": vec(lru_lambda),
        "w_branch_rnn": w_branch_rnn[i].astype(BF16),
        "w_branch_attn": w_branch_attn[i].astype(BF16),
        "w_out": w_out[i].astype(BF16), "mix_post": vec(mix_post),
        "ple_gate": ple_gate[i].astype(BF16), "ple_proj": ple_proj[i].astype(BF16), "ple_post": vec(ple_post),
    }


def _rope_tables(pos, *, prompt):
    half = QK_ROPE // 2
    freqs = ROPE_THETA ** (-jnp.arange(half, dtype=F32) / half)
    ang = pos.astype(F32)[:, None] * freqs[None, :]
    cos32 = jnp.tile(jnp.cos(ang), (1, 2))
    sin32 = jnp.tile(jnp.sin(ang), (1, 2))
    n = pos.shape[0]
    tabs = {"cos_k": _pad_axis(cos32, 1, 0, HEAD_PAD - QK_ROPE), "sin_k": _pad_axis(sin32, 1, 0, HEAD_PAD - QK_ROPE)}
    if prompt:
        tabs["cos_qt"] = Q_SCALE * cos32.T
        tabs["sin_qt"] = Q_SCALE * sin32.T
    else:
        tabs["cos512"] = Q_SCALE * jnp.tile(cos32, (1, N_HEADS))
        tabs["sin512"] = Q_SCALE * jnp.tile(sin32, (1, N_HEADS))
    return tabs


def _tile(n, pref):
    return pref if n % pref == 0 else n


def _layer_prompt(x, p, w):
    b, s, _ = x.shape
    n = b * s
    tabs = _rope_tables(jnp.arange(s, dtype=jnp.int32), prompt=True)
    x1 = _ffn(x.reshape(n, D_MODEL), *w["ffn1"], tm=_tile(n, 512))
    xr, gy, ckv, krope, ga, gb, qt, k, vt = _proj(x1.reshape(b, s, D_MODEL), w, tabs, prompt=True, tm=_tile(s, 256))
    hg, h_last, conv_new = _rglru_prompt(xr, gy, w, ts=_tile(s, 256))
    o = _attn_prompt(qt, k, vt, t=_tile(s, 512))
    y = _back(x1, hg.reshape(n, D_RNN), o.reshape(n, D_MODEL), ga.reshape(n, D_MODEL), gb.reshape(n, D_MODEL),
              p.reshape(n, D_PLE), w, tm=_tile(n, 256))
    return y.reshape(b, s, D_MODEL), (ckv, krope, h_last.reshape(b, D_RNN), conv_new)


def _layer_sample(x, p, h0, conv_buf, pool_ckv, pool_kr, page_table, w):
    n, s, _ = x.shape
    past_len = page_table.shape[1] * PAGE_SIZE
    tabs = _rope_tables(jnp.full((n,), past_len, jnp.int32), prompt=False)
    x1 = _ffn(x.reshape(n, D_MODEL), *w["ffn1"], tm=n)
    xr, gy, ckv, krope, ga, gb, qn, qr = _proj(x1.reshape(1, n, D_MODEL), w, tabs, prompt=False, tm=n)
    hg, h_new, conv_new = _rglru_sample(xr[0], gy[0], jnp.transpose(conv_buf, (1, 0, 2)), h0, w)
    qa = _absorb(qn[0], w["w_ukt3"]).reshape(n, N_HEADS, KV_LORA)
    o_lat = _decode(page_table, qa, qr.reshape(n, N_HEADS, QK_ROPE), ckv.reshape(n, 1, KV_LORA),
                    krope.reshape(n, 1, QK_ROPE), pool_ckv, jnp.transpose(pool_kr, (0, 2, 1)))
    o = _unabsorb(o_lat.reshape(n, N_HEADS * KV_LORA), w["w_uv_pair"])
    y = _back(x1, hg, o, ga[0], gb[0], p.reshape(n, D_PLE), w, tm=n)
    return (y.reshape(n, s, D_MODEL),
            (ckv.reshape(n, s, KV_LORA), krope.reshape(n, s, QK_ROPE), h_new, jnp.transpose(conv_new, (1, 0, 2))))


def kernel(x_prompt, x_sample, p_prompt, p_sample, cache_ckv, cache_krope, state_h, state_conv, page_table,
           ffn1_pre, ffn1_w_gate, ffn1_w_up, ffn1_w_down, ffn1_post, mix_pre, w_in, conv_w, conv_b, lru_w_a,
           lru_b_a, lru_w_i, lru_b_i, lru_lambda, w_branch_rnn, q_norm, w_uq, w_qr, kv_norm, w_uk, w_uv,
           w_branch_attn, w_out, mix_post, ffn2_pre, ffn2_w_gate, ffn2_w_up, ffn2_w_down, ffn2_post, ple_gate,
           ple_proj, ple_post):
    assert x_sample.shape[1] == 1, "the sample group carries one new token per sequence"
    depth = ffn1_pre.shape[0]
    hp, hs = x_prompt, x_sample
    st_p, st_s = [], []
    for i in range(depth):
        w = _prep_layer(i, ffn1_pre, ffn1_w_gate, ffn1_w_up, ffn1_w_down, ffn1_post, mix_pre, w_in, conv_w, conv_b,
                        lru_w_a, lru_b_a, lru_w_i, lru_b_i, lru_lambda, w_branch_rnn, q_norm, w_uq, w_qr, kv_norm,
                        w_uk, w_uv, w_branch_attn, w_out, mix_post, ffn2_pre, ffn2_w_gate, ffn2_w_up, ffn2_w_down,
                        ffn2_post, ple_gate, ple_proj, ple_post)
        hp, sp = _layer_prompt(hp, p_prompt[i], w)
        hs, ss = _layer_sample(hs, p_sample[i], state_h[i], state_conv[i], cache_ckv[i], cache_krope[i],
                               page_table, w)
        st_p.append(sp)
        st_s.append(ss)
    stack = lambda sts, k: jnp.stack([s[k] for s in sts])
    return (hp, hs, stack(st_p, 0), stack(st_p, 1), stack(st_p, 2), stack(st_p, 3),
            stack(st_s, 0), stack(st_s, 1), stack(st_s, 2), stack(st_s, 3))
```

```python
import functools

import jax
import jax.numpy as jnp
from jax import lax
from jax.experimental import pallas as pl
from jax.experimental.pallas import tpu as pltpu

D_MODEL = 1024
D_RNN = 1280
RNN_BLOCKS = 16
RNN_BLOCK = D_RNN // RNN_BLOCKS
CONV_W = 4
LRU_C = 8.0
N_HEADS = 16
QK_NOPE = 64
QK_ROPE = 32
V_HEAD = 64
Q_LORA = 384
KV_LORA = 256
ROPE_THETA = 10000.0
SM_SCALE = (QK_NOPE + QK_ROPE) ** -0.5
Q_SCALE = SM_SCALE * 1.4426950408889634
D_FF = 2816
D_PLE = 256
EPS = 1e-6
PAGE_SIZE = 128
O_XR = D_RNN
O_YR = O_XR + D_RNN
O_Q = O_YR + Q_LORA
O_KV = O_Q + KV_LORA
O_KR = O_KV + QK_ROPE

LANES = 128
HEAD_PAD = LANES
VMEM_LIMIT_BYTES = 56 * 1024 * 1024

F32 = jnp.float32
BF16 = jnp.bfloat16


def _rms(x, g):
    return x * lax.rsqrt(jnp.mean(x * x, axis=-1, keepdims=True) + EPS) * g


def _dot(a, b):
    return jnp.dot(a, b, preferred_element_type=F32)


def _dot_nt(a, b):
    return lax.dot_general(a, b, (((1,), (1,)), ((), ())), preferred_element_type=F32)


def _resident(shape):
    nd = len(shape)
    return pl.BlockSpec(shape, lambda *_: (0,) * nd, pipeline_mode=pl.Buffered(1))


def _params(*sem):
    return pltpu.CompilerParams(dimension_semantics=sem, vmem_limit_bytes=VMEM_LIMIT_BYTES)


def _ffn_block(x, pre_ref, wg_ref, wu_ref, wd_ref, post_ref, f_chunk):
    u = _rms(x, pre_ref[...]).astype(BF16)
    acc = jnp.zeros(x.shape, F32)
    for c in range(D_FF // f_chunk):
        sl = slice(c * f_chunk, (c + 1) * f_chunk)
        g = _dot(u, wg_ref[:, sl])
        h = (g * jax.nn.sigmoid(g)) * _dot(u, wu_ref[:, sl])
        acc = acc + _dot(h.astype(BF16), wd_ref[sl, :])
    return x + 0.5 * _rms(acc, post_ref[...])


def _ffn_kernel(x_ref, pre_ref, wg_ref, wu_ref, wd_ref, post_ref, o_ref, *, f_chunk):
    o_ref[...] = _ffn_block(x_ref[...], pre_ref, wg_ref, wu_ref, wd_ref, post_ref, f_chunk)


def _ffn(x, pre, wg, wu, wd, post, *, tm):
    n = x.shape[0]
    return pl.pallas_call(
        functools.partial(_ffn_kernel, f_chunk=D_FF // 2),
        grid=(n // tm,),
        in_specs=[pl.BlockSpec((tm, D_MODEL), lambda i: (i, 0)),
                  _resident((1, D_MODEL)), _resident((D_MODEL, D_FF)), _resident((D_MODEL, D_FF)),
                  _resident((D_FF, D_MODEL)), _resident((1, D_MODEL))],
        out_specs=pl.BlockSpec((tm, D_MODEL), lambda i: (i, 0)),
        out_shape=jax.ShapeDtypeStruct((n, D_MODEL), F32),
        compiler_params=_params("parallel"),
        name="ffn",
    )(x, pre, wg, wu, wd, post)


N_MAIN = 2 * D_RNN + Q_LORA + KV_LORA + 2 * D_MODEL
M_YR = D_RNN
M_CQ = 2 * D_RNN
M_KV = M_CQ + Q_LORA
M_GA = M_KV + KV_LORA
M_GB = M_GA + D_MODEL


def _proj_kernel(*refs, prompt):
    (x_ref, pre_ref, wmain_ref, qn_ref, kvn_ref, wkr_ref, cosk_ref, sink_ref), refs = refs[:8], refs[8:]
    if prompt:
        (wqrt_ref, wqrtrot_ref, wuqt_ref, cosqt_ref, sinqt_ref, wuk_ref, wuvt_ref), refs = refs[:7], refs[7:]
    else:
        (wuq_ref, wqr_ref, wqrrot_ref, cos512_ref, sin512_ref), refs = refs[:5], refs[5:]
    (xr_ref, gy_ref, ckv_ref, kr_ref, ga_ref, gb_ref), refs = refs[:6], refs[6:]

    u = _rms(x_ref[0], pre_ref[...]).astype(BF16)
    xr_ref[0] = _dot(u, wmain_ref[:, 0:M_YR])
    gy_ref[0] = jax.nn.gelu(_dot(u, wmain_ref[:, M_YR:M_CQ])).astype(BF16)
    ga_ref[0] = jax.nn.sigmoid(_dot(u, wmain_ref[:, M_GA:M_GB])).astype(BF16)
    gb_ref[0] = jax.nn.sigmoid(_dot(u, wmain_ref[:, M_GB:N_MAIN])).astype(BF16)
    cq = _rms(_dot(u, wmain_ref[:, M_CQ:M_KV]), qn_ref[...]).astype(BF16)
    ckv = _rms(_dot(u, wmain_ref[:, M_KV:M_GA]), kvn_ref[...])
    ckv_ref[0] = ckv
    ckv_b = ckv.astype(BF16)
    kr2 = _dot(u, wkr_ref[...])
    kr = kr2[:, :HEAD_PAD] * cosk_ref[...] + kr2[:, HEAD_PAD:] * sink_ref[...]
    kr_ref[0] = kr[:, :QK_ROPE]

    if prompt:
        qt_ref, k_ref, vt_ref = refs
        tm = cq.shape[0]
        k_nope = _dot(ckv_b, wuk_ref[...])
        qr_t = _dot_nt(wqrt_ref[...], cq)
        qrot_t = _dot_nt(wqrtrot_ref[...], cq)
        qn_t = _dot_nt(wuqt_ref[...], cq) * Q_SCALE
        v_t = _dot_nt(wuvt_ref[...], ckv_b)
        cos_t = cosqt_ref[...]
        sin_t = sinqt_ref[...]
        q_pad = jnp.zeros((HEAD_PAD - QK_ROPE - QK_NOPE, tm), BF16)
        ones_row = (lax.broadcasted_iota(jnp.int32, (HEAD_PAD - V_HEAD, tm), 0) == 0).astype(BF16)
        for h in range(N_HEADS):
            r0 = h * HEAD_PAD
            k_ref[0, h] = (k_nope[:, r0:r0 + HEAD_PAD] + kr).astype(BF16)
            rope = slice(h * QK_ROPE, (h + 1) * QK_ROPE)
            qt_ref[0, r0:r0 + QK_ROPE, :] = (qr_t[rope] * cos_t + qrot_t[rope] * sin_t).astype(BF16)
            qt_ref[0, r0 + QK_ROPE:r0 + QK_ROPE + QK_NOPE, :] = qn_t[h * QK_NOPE:(h + 1) * QK_NOPE].astype(BF16)
            qt_ref[0, r0 + QK_ROPE + QK_NOPE:r0 + HEAD_PAD, :] = q_pad
            vt_ref[0, r0:r0 + V_HEAD, :] = v_t[h * V_HEAD:(h + 1) * V_HEAD].astype(BF16)
            vt_ref[0, r0 + V_HEAD:r0 + HEAD_PAD, :] = ones_row
    else:
        qn_out_ref, qr_out_ref = refs
        qn_out_ref[0] = (_dot(cq, wuq_ref[...]) * Q_SCALE).astype(BF16)
        qr_out_ref[0] = (_dot(cq, wqr_ref[...]) * cos512_ref[...]
                         + _dot(cq, wqrrot_ref[...]) * sin512_ref[...]).astype(BF16)


def _proj(x, w, tabs, *, prompt, tm):
    b, s, _ = x.shape
    row = lambda d: pl.BlockSpec((1, tm, d), lambda bi, i: (bi, i, 0))
    tab = lambda d: pl.BlockSpec((tm, d), lambda bi, i: (i, 0))
    tab_t = pl.BlockSpec((QK_ROPE, tm), lambda bi, i: (0, i))
    in_specs = [row(D_MODEL), _resident((1, D_MODEL)), _resident((D_MODEL, N_MAIN)), _resident((1, Q_LORA)),
                _resident((1, KV_LORA)), _resident((D_MODEL, 2 * HEAD_PAD)), tab(HEAD_PAD), tab(HEAD_PAD)]
    args = [x, w["mix_pre"], w["w_main"], w["q_norm"], w["kv_norm"], w["w_kr2"], tabs["cos_k"], tabs["sin_k"]]
    out_specs = [row(D_RNN), row(D_RNN), row(KV_LORA), row(QK_ROPE), row(D_MODEL), row(D_MODEL)]
    out_shape = [jax.ShapeDtypeStruct((b, s, D_RNN), F32), jax.ShapeDtypeStruct((b, s, D_RNN), BF16),
                 jax.ShapeDtypeStruct((b, s, KV_LORA), F32), jax.ShapeDtypeStruct((b, s, QK_ROPE), F32),
                 jax.ShapeDtypeStruct((b, s, D_MODEL), BF16), jax.ShapeDtypeStruct((b, s, D_MODEL), BF16)]
    hp = N_HEADS * HEAD_PAD
    if prompt:
        in_specs += [_resident((N_HEADS * QK_ROPE, Q_LORA)), _resident((N_HEADS * QK_ROPE, Q_LORA)),
                     _resident((N_HEADS * QK_NOPE, Q_LORA)), tab_t, tab_t,
                     _resident((KV_LORA, hp)), _resident((N_HEADS * V_HEAD, KV_LORA))]
        args += [w["w_qrt"], w["w_qrt_rot"], w["w_uqt"], tabs["cos_qt"], tabs["sin_qt"], w["w_uk_pad"], w["w_uvt"]]
        lanes_major = pl.BlockSpec((1, hp, tm), lambda bi, i: (bi, 0, i))
        out_specs += [lanes_major, pl.BlockSpec((1, N_HEADS, tm, HEAD_PAD), lambda bi, i: (bi, 0, i, 0)), lanes_major]
        out_shape += [jax.ShapeDtypeStruct((b, hp, s), BF16),
                      jax.ShapeDtypeStruct((b, N_HEADS, s, HEAD_PAD), BF16),
                      jax.ShapeDtypeStruct((b, hp, s), BF16)]
    else:
        nn, nr = N_HEADS * QK_NOPE, N_HEADS * QK_ROPE
        in_specs += [_resident((Q_LORA, nn)), _resident((Q_LORA, nr)), _resident((Q_LORA, nr)), tab(nr), tab(nr)]
        args += [w["w_uq"], w["w_qr"], w["w_qr_rot"], tabs["cos512"], tabs["sin512"]]
        out_specs += [row(nn), row(nr)]
        out_shape += [jax.ShapeDtypeStruct((b, s, nn), BF16), jax.ShapeDtypeStruct((b, s, nr), BF16)]
    return pl.pallas_call(
        functools.partial(_proj_kernel, prompt=prompt),
        grid=(b, s // tm),
        in_specs=in_specs, out_specs=out_specs, out_shape=out_shape,
        compiler_params=_params("parallel", "parallel"),
        name="proj_prompt" if prompt else "proj_sample",
    )(*args)


def _softplus(x):
    return jnp.maximum(x, 0.0) + jnp.log1p(jnp.exp(-jnp.abs(x)))


def _sigmoid(z):
    return 0.5 * jnp.tanh(0.5 * z) + 0.5


def _lru_coeffs(xc, wbd_ref, bab_ref, lam_ref):
    z = _dot(xc.astype(BF16), wbd_ref[...]) + bab_ref[...]
    r = _sigmoid(z[:, :D_RNN])
    gi = _sigmoid(z[:, D_RNN:])
    log_a = -LRU_C * r * _softplus(-lam_ref[...])
    a = jnp.exp(log_a)
    th = jnp.tanh(log_a)
    mult = jnp.sqrt(-2.0 * th / (1.0 - th))
    return a, mult * (gi * xc)


SUBLANES = 8


def _shift_rows(x, tail, k):
    rolled = pltpu.roll(x, k, axis=0)
    head_rows = lax.broadcasted_iota(jnp.int32, tail.shape, 0)
    first = jnp.where(head_rows < k, pltpu.roll(tail, k, axis=0), rolled[:SUBLANES])
    return jnp.concatenate([first, rolled[SUBLANES:]], axis=0)


def _rglru_prompt_kernel(xr_ref, gy_ref, cw_ref, cb_ref, wbd_ref, bab_ref, lam_ref,
                         hg_ref, hlast_ref, convnew_ref, tail_s, a_s, b_s, h_s, hcar, *, ts):
    t = pl.program_id(0)
    nb = xr_ref.shape[0]

    @pl.when(t == 0)
    def _():
        tail_s[...] = jnp.zeros(tail_s.shape, F32)
        hcar[...] = jnp.zeros(hcar.shape, F32)

    for bi in range(nb):
        x = xr_ref[bi]
        tail = tail_s[bi]
        xc = cb_ref[...] + _shift_rows(x, tail, 3) * cw_ref[0:1, :]
        xc = xc + _shift_rows(x, tail, 2) * cw_ref[1:2, :]
        xc = xc + _shift_rows(x, tail, 1) * cw_ref[2:3, :]
        xc = xc + x * cw_ref[3:4, :]
        tail_s[bi] = x[ts - SUBLANES:, :]
        a, b = _lru_coeffs(xc, wbd_ref, bab_ref, lam_ref)
        a_s[bi] = a
        b_s[bi] = b

    def step(i, hs):
        out = []
        for bi in range(nb):
            h = a_s[bi, pl.ds(i, 1), :] * hs[bi] + b_s[bi, pl.ds(i, 1), :]
            h_s[bi, pl.ds(i, 1), :] = h
            out.append(h)
        return tuple(out)

    hs = lax.fori_loop(0, ts, step, tuple(hcar[bi] for bi in range(nb)), unroll=8)
    for bi in range(nb):
        hcar[bi] = hs[bi]
        hg_ref[bi] = (h_s[bi] * gy_ref[bi].astype(F32)).astype(BF16)

    @pl.when(t == pl.num_programs(0) - 1)
    def _():
        for bi in range(nb):
            hlast_ref[bi] = hs[bi]
            convnew_ref[bi] = xr_ref[bi, ts - (CONV_W - 1):, :]


def _rglru_prompt(xr, gy, w, *, ts):
    b, s, _ = xr.shape
    rows = pl.BlockSpec((b, ts, D_RNN), lambda t: (0, t, 0))
    whole = lambda d: pl.BlockSpec((b, d, D_RNN), lambda t: (0, 0, 0))
    return pl.pallas_call(
        functools.partial(_rglru_prompt_kernel, ts=ts),
        grid=(s // ts,),
        in_specs=[rows, rows, _resident((CONV_W, D_RNN)), _resident((1, D_RNN)), _resident((D_RNN, 2 * D_RNN)),
                  _resident((1, 2 * D_RNN)), _resident((1, D_RNN))],
        out_specs=[rows, whole(1), whole(CONV_W - 1)],
        out_shape=[jax.ShapeDtypeStruct((b, s, D_RNN), BF16), jax.ShapeDtypeStruct((b, 1, D_RNN), F32),
                   jax.ShapeDtypeStruct((b, CONV_W - 1, D_RNN), F32)],
        scratch_shapes=[pltpu.VMEM((b, SUBLANES, D_RNN), F32), pltpu.VMEM((b, ts, D_RNN), F32),
                        pltpu.VMEM((b, ts, D_RNN), F32), pltpu.VMEM((b, ts, D_RNN), F32),
                        pltpu.VMEM((b, 1, D_RNN), F32)],
        compiler_params=_params("arbitrary"),
        name="rglru_prompt",
    )(xr, gy, w["conv_w"], w["conv_b"], w["w_lru"], w["b_lru"], w["lam"])


def _rglru_sample_kernel(xr_ref, gy_ref, sc_ref, h0_ref, cw_ref, cb_ref, wbd_ref, bab_ref, lam_ref,
                         hg_ref, hnew_ref, convnew_ref):
    x = xr_ref[...]
    xc = cb_ref[...] + sc_ref[0] * cw_ref[0:1, :]
    xc = xc + sc_ref[1] * cw_ref[1:2, :]
    xc = xc + sc_ref[2] * cw_ref[2:3, :]
    xc = xc + x * cw_ref[3:4, :]
    a, b = _lru_coeffs(xc, wbd_ref, bab_ref, lam_ref)
    h = a * h0_ref[...] + b
    hnew_ref[...] = h
    hg_ref[...] = (h * gy_ref[...].astype(F32)).astype(BF16)
    convnew_ref[0] = sc_ref[1]
    convnew_ref[1] = sc_ref[2]
    convnew_ref[2] = x


def _rglru_sample(xr, gy, sc, h0, w):
    n = xr.shape[0]
    return pl.pallas_call(
        _rglru_sample_kernel,
        out_shape=[jax.ShapeDtypeStruct((n, D_RNN), BF16), jax.ShapeDtypeStruct((n, D_RNN), F32),
                   jax.ShapeDtypeStruct((CONV_W - 1, n, D_RNN), F32)],
        compiler_params=pltpu.CompilerParams(vmem_limit_bytes=VMEM_LIMIT_BYTES),
        name="rglru_sample",
    )(xr, gy, sc, h0, w["conv_w"], w["conv_b"], w["w_lru"], w["b_lru"], w["lam"])


HEADS_PER_STEP = 4


def _attn_prompt_kernel(qt_ref, k_ref, vt_ref, o_ref, m_s, acc_s, s_a, s_b, *, t):
    qq = pl.program_id(2)
    heads = range(HEADS_PER_STEP)
    m_s[...] = jnp.full(m_s.shape, -jnp.inf, F32)
    acc_s[...] = jnp.zeros(acc_s.shape, F32)

    def rows(hh):
        return slice(hh * HEAD_PAD, (hh + 1) * HEAD_PAD)

    def scores(stage, j, hh):
        k0 = pl.multiple_of(j * t, t)
        return _dot(k_ref[0, hh, pl.ds(k0, t), :], qt_ref[0, rows(hh), stage * t:(stage + 1) * t])

    def consume(stage, s, j, hh, masked=False):
        k0 = pl.multiple_of(j * t, t)
        if masked:
            kpos = lax.broadcasted_iota(jnp.int32, (t, t), 0)
            qpos = lax.broadcasted_iota(jnp.int32, (t, t), 1)
            s = jnp.where(kpos <= qpos, s, -jnp.inf)
        m_old = m_s[stage, hh]
        m_new = jnp.maximum(m_old, jnp.max(s, axis=0, keepdims=True))
        p = jnp.exp2(s - m_new).astype(BF16)
        acc_s[stage, hh] = (jnp.exp2(m_old - m_new) * acc_s[stage, hh]
                            + _dot(vt_ref[0, rows(hh), pl.ds(k0, t)], p))
        m_s[stage, hh] = m_new

    def step(stage, j, src, dst, nxt):
        for hh in heads:
            dst[hh] = scores(nxt[0], nxt[1], hh)
            consume(stage, src[hh], j, hh)

    def finish(stage):
        o_t = jnp.concatenate(
            [acc_s[stage, hh][:V_HEAD] * (1.0 / acc_s[stage, hh][V_HEAD:V_HEAD + 1]) for hh in heads], axis=0)
        o_ref[0, stage * t:(stage + 1) * t, :] = o_t.T.astype(BF16)

    for hh in heads:
        s_a[hh] = scores(0, 0, hh)

    def pair0(i, c):
        step(0, 2 * i, s_a, s_b, (0, 2 * i + 1))
        step(0, 2 * i + 1, s_b, s_a, (0, 2 * i + 2))
        return c

    def run_pairs(pair):
        lax.fori_loop(0, qq // 2, lambda i, c: pair(2 * i + 1, pair(2 * i, c)), 0)

        @pl.when(qq % 2 == 1)
        def _():
            pair(qq - 1, 0)

    run_pairs(pair0)
    for hh in heads:
        s_b[hh] = scores(1, 0, hh)
        consume(0, s_a[hh], 2 * qq, hh, masked=True)
    finish(0)

    def pair1(i, c):
        step(1, 2 * i, s_b, s_a, (1, 2 * i + 1))
        step(1, 2 * i + 1, s_a, s_b, (1, 2 * i + 2))
        return c

    run_pairs(pair1)
    step(1, 2 * qq, s_b, s_a, (1, 2 * qq + 1))
    for hh in heads:
        consume(1, s_a[hh], 2 * qq + 1, hh, masked=True)
    finish(1)


def _attn_prompt(qt, k, vt, *, t):
    b, h, s, _ = k.shape
    hps = HEADS_PER_STEP
    assert (hps * V_HEAD) % LANES == 0 and h % hps == 0 and s % (2 * t) == 0
    return pl.pallas_call(
        functools.partial(_attn_prompt_kernel, t=t),
        grid=(b, h // hps, s // (2 * t)),
        in_specs=[pl.BlockSpec((1, hps * HEAD_PAD, 2 * t), lambda bi, hi, i: (bi, hi, i)),
                  pl.BlockSpec((1, hps, s, HEAD_PAD), lambda bi, hi, i: (bi, hi, 0, 0)),
                  pl.BlockSpec((1, hps * HEAD_PAD, s), lambda bi, hi, i: (bi, hi, 0))],
        out_specs=pl.BlockSpec((1, 2 * t, hps * V_HEAD), lambda bi, hi, i: (bi, i, hi)),
        out_shape=jax.ShapeDtypeStruct((b, s, h * V_HEAD), BF16),
        scratch_shapes=[pltpu.VMEM((2, hps, 1, t), F32), pltpu.VMEM((2, hps, HEAD_PAD, t), F32),
                        pltpu.VMEM((hps, t, t), F32), pltpu.VMEM((hps, t, t), F32)],
        compiler_params=_params("parallel", "parallel", "arbitrary"),
        name="attn_prompt",
    )(qt, k, vt)


def _absorb_kernel(qn_ref, wukt_ref, qa_ref):
    qn = qn_ref[...]
    for h in range(N_HEADS):
        qa_ref[:, h * KV_LORA:(h + 1) * KV_LORA] = _dot(
            qn[:, h * QK_NOPE:(h + 1) * QK_NOPE], wukt_ref[h]).astype(BF16)


def _absorb(qn, wukt3):
    n = qn.shape[0]
    return pl.pallas_call(
        _absorb_kernel,
        out_shape=jax.ShapeDtypeStruct((n, N_HEADS * KV_LORA), BF16),
        compiler_params=pltpu.CompilerParams(vmem_limit_bytes=VMEM_LIMIT_BYTES),
        name="absorb",
    )(qn, wukt3)


def _decode_kernel(pt_ref, qa_ref, qr_ref, cn_ref, kn_ref, pool_ckv, pool_krt, o_ref,
                   ckv_buf, krt_buf, page_a, page_b, p_a, p_b, tail_a, tail_b, l_a, l_b, sems, *, n_seq, n_pages):
    b = pl.program_id(0)
    slot = lax.rem(b, 2)
    set_a = (page_a, p_a, tail_a, l_a)
    set_b = (page_b, p_b, tail_b, l_b)

    def page_copies(page, sl, p):
        return (pltpu.make_async_copy(pool_ckv.at[page], ckv_buf.at[sl, p], sems.at[sl, 0]),
                pltpu.make_async_copy(pool_krt.at[page], krt_buf.at[sl, p], sems.at[sl, 1]))

    def start_row(row, sl):
        def body(p, c):
            for cp in page_copies(pt_ref[row * n_pages + p], sl, p):
                cp.start()
            return c
        lax.fori_loop(0, n_pages, body, 0)

    @pl.when(b == 0)
    def _():
        start_row(0, 0)
        page_b[...] = jnp.zeros(page_b.shape, BF16)
        p_b[...] = jnp.zeros(p_b.shape, BF16)
        tail_b[...] = jnp.zeros(tail_b.shape, F32)
        l_b[...] = jnp.ones(l_b.shape, F32)

    @pl.when(b + 1 < n_seq)
    def _():
        start_row(b + 1, 1 - slot)

    @pl.when(b < n_seq)
    def _():
        for p in range(n_pages):
            for cp in page_copies(0, slot, p):
                cp.wait()

    def score(cur):
        page_c, p_c, tail_c, l_c = cur
        qa = qa_ref[0]
        qr = qr_ref[0]
        s_pairs = []
        for p in range(0, n_pages, 2):
            pair = jnp.concatenate([ckv_buf[slot, p], ckv_buf[slot, p + 1]], axis=0).astype(BF16)
            page_c[p] = pair[:PAGE_SIZE]
            page_c[p + 1] = pair[PAGE_SIZE:]
            kr_pair = jnp.concatenate([krt_buf[slot, p], krt_buf[slot, p + 1]], axis=1).astype(BF16)
            s_pairs.append(_dot_nt(qa, pair) + _dot(qr, kr_pair))
        s = jnp.concatenate(s_pairs, axis=1)
        cn = cn_ref[0].astype(BF16).astype(F32)
        kn = kn_ref[0].astype(BF16).astype(F32)
        s_new = (jnp.sum(qa.astype(F32) * cn, axis=-1, keepdims=True)
                 + jnp.sum(qr.astype(F32) * kn, axis=-1, keepdims=True))
        m = jnp.maximum(jnp.max(s, axis=-1, keepdims=True), s_new)
        p_past = jnp.exp2(s - m)
        p_new = jnp.exp2(s_new - m)
        l_c[...] = jnp.sum(p_past, axis=-1, keepdims=True) + p_new
        tail_c[...] = p_new * cn
        p_c[...] = p_past.astype(BF16)

    def attend(prv):
        page_p, p_p, tail_p, l_p = prv
        acc = tail_p[...]
        for p in range(n_pages):
            acc = acc + _dot(p_p[:, p * PAGE_SIZE:(p + 1) * PAGE_SIZE], page_p[p])
        o_ref[0] = (acc / l_p[...]).astype(BF16)

    @pl.when(jnp.logical_and(b < n_seq, slot == 0))
    def _():
        score(set_a)
        attend(set_b)

    @pl.when(jnp.logical_and(b < n_seq, slot == 1))
    def _():
        score(set_b)
        attend(set_a)

    @pl.when(b == n_seq)
    def _():
        attend(set_a if (n_seq - 1) % 2 == 0 else set_b)


def _decode(page_table, qa, qr, ckv_new, kr_new, pool_ckv, pool_krt):
    n, n_pages = page_table.shape
    per_row = lambda d0, d1: pl.BlockSpec((1, d0, d1), lambda bi, pt: (jnp.minimum(bi, n - 1), 0, 0))
    hbm = pl.BlockSpec(memory_space=pl.ANY)
    page_set = [pltpu.VMEM((n_pages, PAGE_SIZE, KV_LORA), BF16)] * 2
    grid_spec = pltpu.PrefetchScalarGridSpec(
        num_scalar_prefetch=1,
        grid=(n + 1,),
        in_specs=[per_row(N_HEADS, KV_LORA), per_row(N_HEADS, QK_ROPE), per_row(1, KV_LORA), per_row(1, QK_ROPE),
                  hbm, hbm],
        out_specs=pl.BlockSpec((1, N_HEADS, KV_LORA), lambda bi, pt: (jnp.maximum(bi - 1, 0), 0, 0)),
        scratch_shapes=[pltpu.VMEM((2, n_pages, PAGE_SIZE, KV_LORA), F32),
                        pltpu.VMEM((2, n_pages, QK_ROPE, PAGE_SIZE), F32),
                        *page_set,
                        *([pltpu.VMEM((N_HEADS, n_pages * PAGE_SIZE), BF16)] * 2),
                        *([pltpu.VMEM((N_HEADS, KV_LORA), F32)] * 2),
                        *([pltpu.VMEM((N_HEADS, 1), F32)] * 2),
                        pltpu.SemaphoreType.DMA((2, 2))],
    )
    return pl.pallas_call(
        functools.partial(_decode_kernel, n_seq=n, n_pages=n_pages),
        grid_spec=grid_spec,
        out_shape=jax.ShapeDtypeStruct((n, N_HEADS, KV_LORA), BF16),
        compiler_params=_params("arbitrary"),
        name="decode",
    )(page_table.reshape(-1), qa, qr, ckv_new, kr_new, pool_ckv, pool_krt)


def _unabsorb_kernel(ol_ref, wuv_ref, o_ref):
    for h2 in range(N_HEADS // 2):
        pair = jnp.zeros((ol_ref.shape[0], LANES), F32)
        for h in (2 * h2, 2 * h2 + 1):
            pair = pair + _dot(ol_ref[:, h * KV_LORA:(h + 1) * KV_LORA], wuv_ref[h])
        o_ref[:, h2 * LANES:(h2 + 1) * LANES] = pair.astype(BF16)


def _unabsorb(o_lat, w_uv_pair):
    n = o_lat.shape[0]
    return pl.pallas_call(
        _unabsorb_kernel,
        out_shape=jax.ShapeDtypeStruct((n, N_HEADS * V_HEAD), BF16),
        compiler_params=pltpu.CompilerParams(vmem_limit_bytes=VMEM_LIMIT_BYTES),
        name="unabsorb",
    )(o_lat, w_uv_pair)


def _back_kernel(x_ref, hg_ref, o_ref, ga_ref, gb_ref, p_ref, wrnn_ref, wattn_ref, wout_ref, mpost_ref,
                 fpre_ref, wg_ref, wu_ref, wd_ref, fpost_ref, pg_ref, pp_ref, ppost_ref, y_ref, *, f_chunk):
    y_a = _dot(hg_ref[...], wrnn_ref[...])
    y_b = _dot(o_ref[...], wattn_ref[...])
    m = ga_ref[...].astype(F32) * y_a + gb_ref[...].astype(F32) * y_b
    x = x_ref[...] + _rms(_dot(m.astype(BF16), wout_ref[...]), mpost_ref[...])
    x = _ffn_block(x, fpre_ref, wg_ref, wu_ref, wd_ref, fpost_ref, f_chunk)
    e = jax.nn.sigmoid(_dot(x.astype(BF16), pg_ref[...])) * _dot(p_ref[...].astype(BF16), pp_ref[...])
    y_ref[...] = x + _rms(e, ppost_ref[...])


def _back(x, hg, o, ga, gb, p, w, *, tm):
    n = x.shape[0]
    row = lambda d: pl.BlockSpec((tm, d), lambda i: (i, 0))
    weights = [w["w_branch_rnn"], w["w_branch_attn"], w["w_out"], w["mix_post"], *w["ffn2"],
               w["ple_gate"], w["ple_proj"], w["ple_post"]]
    return pl.pallas_call(
        functools.partial(_back_kernel, f_chunk=D_FF // 2),
        grid=(n // tm,),
        in_specs=[row(D_MODEL), row(D_RNN), row(D_MODEL), row(D_MODEL), row(D_MODEL), row(D_PLE)]
                 + [_resident(a.shape) for a in weights],
        out_specs=row(D_MODEL),
        out_shape=jax.ShapeDtypeStruct((n, D_MODEL), F32),
        compiler_params=_params("parallel"),
        name="back",
    )(x, hg, o, ga, gb, p, *weights)


def _rot_cols(w):
    half = QK_ROPE // 2
    return jnp.concatenate([-w[..., half:], w[..., :half]], axis=-1)


def _pad_axis(a, axis, before, after):
    pads = [(0, 0)] * a.ndim
    pads[axis] = (before, after)
    return jnp.pad(a, pads)


def _prep_layer(i, ffn1_pre, ffn1_w_gate, ffn1_w_up, ffn1_w_down, ffn1_post, mix_pre, w_in, conv_w, conv_b,
                lru_w_a, lru_b_a, lru_w_i, lru_b_i, lru_lambda, w_branch_rnn, q_norm, w_uq, w_qr, kv_norm, w_uk,
                w_uv, w_branch_attn, w_out, mix_post, ffn2_pre, ffn2_w_gate, ffn2_w_up, ffn2_w_down, ffn2_post,
                ple_gate, ple_proj, ple_post):
    vec = lambda a: a[i].reshape(1, -1)
    w_in_i = w_in[i]
    w_kr = w_in_i[:, O_KV:O_KR]
    pad_r = HEAD_PAD - QK_ROPE - QK_NOPE
    w_qr_i, w_uq_i = w_qr[i], w_uq[i]
    same_block = (jnp.arange(D_RNN)[:, None] // RNN_BLOCK) == (jnp.arange(D_RNN)[None, :] // RNN_BLOCK)
    blockdiag = lambda wb: jnp.where(same_block, jnp.tile(wb.reshape(D_RNN, RNN_BLOCK), (1, RNN_BLOCKS)), 0.0)
    return {
        "ffn1": (vec(ffn1_pre), ffn1_w_gate[i].astype(BF16), ffn1_w_up[i].astype(BF16),
                 ffn1_w_down[i].astype(BF16), vec(ffn1_post)),
        "ffn2": (vec(ffn2_pre), ffn2_w_gate[i].astype(BF16), ffn2_w_up[i].astype(BF16),
                 ffn2_w_down[i].astype(BF16), vec(ffn2_post)),
        "mix_pre": vec(mix_pre),
        "w_main": jnp.concatenate([w_in_i[:, :O_KV], w_in_i[:, O_KR:]], axis=1).astype(BF16),
        "w_kr2": jnp.concatenate([_pad_axis(w_kr, 1, 0, HEAD_PAD - QK_ROPE),
                                  _pad_axis(_rot_cols(w_kr), 1, 0, HEAD_PAD - QK_ROPE)], axis=1).astype(BF16),
        "q_norm": vec(q_norm), "kv_norm": vec(kv_norm),
        "w_qrt": jnp.transpose(w_qr_i, (1, 2, 0)).reshape(N_HEADS * QK_ROPE, Q_LORA).astype(BF16),
        "w_qrt_rot": jnp.transpose(_rot_cols(w_qr_i), (1, 2, 0)).reshape(N_HEADS * QK_ROPE, Q_LORA).astype(BF16),
        "w_uqt": jnp.transpose(w_uq_i, (1, 2, 0)).reshape(N_HEADS * QK_NOPE, Q_LORA).astype(BF16),
        "w_uk_pad": _pad_axis(w_uk[i], 2, QK_ROPE, pad_r).reshape(KV_LORA, N_HEADS * HEAD_PAD).astype(BF16),
        "w_uvt": jnp.transpose(w_uv[i], (1, 2, 0)).reshape(N_HEADS * V_HEAD, KV_LORA).astype(BF16),
        "w_uv_pair": jnp.transpose(
            jnp.where((jnp.arange(N_HEADS) % 2 == 1)[None, :, None], _pad_axis(w_uv[i], 2, V_HEAD, 0),
                      _pad_axis(w_uv[i], 2, 0, V_HEAD)), (1, 0, 2)).astype(BF16),
        "w_uq": w_uq_i.reshape(Q_LORA, N_HEADS * QK_NOPE).astype(BF16),
        "w_qr": w_qr_i.reshape(Q_LORA, N_HEADS * QK_ROPE).astype(BF16),
        "w_qr_rot": _rot_cols(w_qr_i).reshape(Q_LORA, N_HEADS * QK_ROPE).astype(BF16),
        "w_ukt3": jnp.transpose(w_uk[i], (1, 2, 0)).astype(BF16),
        "conv_w": conv_w[i], "conv_b": vec(conv_b),
        "w_lru": jnp.concatenate([blockdiag(lru_w_a[i]), blockdiag(lru_w_i[i])], axis=1).astype(BF16),
        "b_lru": jnp.concatenate([lru_b_a[i].reshape(1, -1), lru_b_i[i].reshape(1, -1)], axis=1),
        "lam": vec(lru_lambda),
        "w_branch_rnn": w_branch_rnn[i].astype(BF16),
        "w_branch_attn": w_branch_attn[i].astype(BF16),
        "w_out": w_out[i].astype(BF16), "mix_post": vec(mix_post),
        "ple_gate": ple_gate[i].astype(BF16), "ple_proj": ple_proj[i].astype(BF16), "ple_post": vec(ple_post),
    }


def _rope_tables(pos, *, prompt):
    half = QK_ROPE // 2
    freqs = ROPE_THETA ** (-jnp.arange(half, dtype=F32) / half)
    ang = pos.astype(F32)[:, None] * freqs[None, :]
    cos32 = jnp.tile(jnp.cos(ang), (1, 2))
    sin32 = jnp.tile(jnp.sin(ang), (1, 2))
    n = pos.shape[0]
    tabs = {"cos_k": _pad_axis(cos32, 1, 0, HEAD_PAD - QK_ROPE), "sin_k": _pad_axis(sin32, 1, 0, HEAD_PAD - QK_ROPE)}
    if prompt:
        tabs["cos_qt"] = Q_SCALE * cos32.T
        tabs["sin_qt"] = Q_SCALE * sin32.T
    else:
        tabs["cos512"] = Q_SCALE * jnp.tile(cos32, (1, N_HEADS))
        tabs["sin512"] = Q_SCALE * jnp.tile(sin32, (1, N_HEADS))
    return tabs


def _tile(n, pref):
    return pref if n % pref == 0 else n


def _layer_prompt(x, p, w):
    b, s, _ = x.shape
    n = b * s
    tabs = _rope_tables(jnp.arange(s, dtype=jnp.int32), prompt=True)
    x1 = _ffn(x.reshape(n, D_MODEL), *w["ffn1"], tm=_tile(n, 1024))
    xr, gy, ckv, krope, ga, gb, qt, k, vt = _proj(x1.reshape(b, s, D_MODEL), w, tabs, prompt=True, tm=_tile(s, 256))
    hg, h_last, conv_new = _rglru_prompt(xr, gy, w, ts=_tile(s, 256))
    o = _attn_prompt(qt, k, vt, t=_tile(s, 512))
    y = _back(x1, hg.reshape(n, D_RNN), o.reshape(n, D_MODEL), ga.reshape(n, D_MODEL), gb.reshape(n, D_MODEL),
              p.reshape(n, D_PLE), w, tm=_tile(n, 512))
    return y.reshape(b, s, D_MODEL), (ckv, krope, h_last.reshape(b, D_RNN), conv_new)


def _layer_sample(x, p, h0, conv_buf, pool_ckv, pool_kr, page_table, w):
    n, s, _ = x.shape
    past_len = page_table.shape[1] * PAGE_SIZE
    tabs = _rope_tables(jnp.full((n,), past_len, jnp.int32), prompt=False)
    x1 = _ffn(x.reshape(n, D_MODEL), *w["ffn1"], tm=n)
    xr, gy, ckv, krope, ga, gb, qn, qr = _proj(x1.reshape(1, n, D_MODEL), w, tabs, prompt=False, tm=n)
    hg, h_new, conv_new = _rglru_sample(xr[0], gy[0], jnp.transpose(conv_buf, (1, 0, 2)), h0, w)
    qa = _absorb(qn[0], w["w_ukt3"]).reshape(n, N_HEADS, KV_LORA)
    o_lat = _decode(page_table, qa, qr.reshape(n, N_HEADS, QK_ROPE), ckv.reshape(n, 1, KV_LORA),
                    krope.reshape(n, 1, QK_ROPE), pool_ckv, jnp.transpose(pool_kr, (0, 2, 1)))
    o = _unabsorb(o_lat.reshape(n, N_HEADS * KV_LORA), w["w_uv_pair"])
    y = _back(x1, hg, o, ga[0], gb[0], p.reshape(n, D_PLE), w, tm=n)
    return (y.reshape(n, s, D_MODEL),
            (ckv.reshape(n, s, KV_LORA), krope.reshape(n, s, QK_ROPE), h_new, jnp.transpose(conv_new, (1, 0, 2))))


def kernel(x_prompt, x_sample, p_prompt, p_sample, cache_ckv, cache_krope, state_h, state_conv, page_table,
           ffn1_pre, ffn1_w_gate, ffn1_w_up, ffn1_w_down, ffn1_post, mix_pre, w_in, conv_w, conv_b, lru_w_a,
           lru_b_a, lru_w_i, lru_b_i, lru_lambda, w_branch_rnn, q_norm, w_uq, w_qr, kv_norm, w_uk, w_uv,
           w_branch_attn, w_out, mix_post, ffn2_pre, ffn2_w_gate, ffn2_w_up, ffn2_w_down, ffn2_post, ple_gate,
           ple_proj, ple_post):
    assert x_sample.shape[1] == 1, "the sample group carries one new token per sequence"
    depth = ffn1_pre.shape[0]
    hp, hs = x_prompt, x_sample
    st_p, st_s = [], []
    for i in range(depth):
        w = _prep_layer(i, ffn1_pre, ffn1_w_gate, ffn1_w_up, ffn1_w_down, ffn1_post, mix_pre, w_in, conv_w, conv_b,
                        lru_w_a, lru_b_a, lru_w_i, lru_b_i, lru_lambda, w_branch_rnn, q_norm, w_uq, w_qr, kv_norm,
                        w_uk, w_uv, w_branch_attn, w_out, mix_post, ffn2_pre, ffn2_w_gate, ffn2_w_up, ffn2_w_down,
                        ffn2_post, ple_gate, ple_proj, ple_post)
        hp, sp = _layer_prompt(hp, p_prompt[i], w)
        hs, ss = _layer_sample(hs, p_sample[i], state_h[i], state_conv[i], cache_ckv[i], cache_krope[i],
                               page_table, w)
        st_p.append(sp)
        st_s.append(ss)
    stack = lambda sts, k: jnp.stack([s[k] for s in sts])
    return (hp, hs, stack(st_p, 0), stack(st_p, 1), stack(st_p, 2), stack(st_p, 3),
            stack(st_s, 0), stack(st_s, 1), stack(st_s, 2), stack(st_s, 3))
```

```python
import functools

import jax
import jax.numpy as jnp
from jax import lax
from jax.experimental import pallas as pl
from jax.experimental.pallas import tpu as pltpu

D_MODEL = 1024
D_RNN = 1280
RNN_BLOCKS = 16
RNN_BLOCK = D_RNN // RNN_BLOCKS
CONV_W = 4
LRU_C = 8.0
N_HEADS = 16
QK_NOPE = 64
QK_ROPE = 32
V_HEAD = 64
Q_LORA = 384
KV_LORA = 256
ROPE_THETA = 10000.0
SM_SCALE = (QK_NOPE + QK_ROPE) ** -0.5
Q_SCALE = SM_SCALE * 1.4426950408889634
D_FF = 2816
D_PLE = 256
EPS = 1e-6
PAGE_SIZE = 128
O_XR = D_RNN
O_YR = O_XR + D_RNN
O_Q = O_YR + Q_LORA
O_KV = O_Q + KV_LORA
O_KR = O_KV + QK_ROPE

LANES = 128
HEAD_PAD = LANES
VMEM_LIMIT_BYTES = 56 * 1024 * 1024

F32 = jnp.float32
BF16 = jnp.bfloat16


def _rms(x, g):
    return x * lax.rsqrt(jnp.mean(x * x, axis=-1, keepdims=True) + EPS) * g


def _dot(a, b):
    return jnp.dot(a, b, preferred_element_type=F32)


def _dot_nt(a, b):
    return lax.dot_general(a, b, (((1,), (1,)), ((), ())), preferred_element_type=F32)


def _resident(shape):
    nd = len(shape)
    return pl.BlockSpec(shape, lambda *_: (0,) * nd, pipeline_mode=pl.Buffered(1))


def _params(*sem):
    return pltpu.CompilerParams(dimension_semantics=sem, vmem_limit_bytes=VMEM_LIMIT_BYTES)


MXU_WIDTH = 256
FFN_SPLIT = (D_FF // MXU_WIDTH + 1) // 2 * MXU_WIDTH


def _ffn_block(x, pre_ref, wg_ref, wu_ref, wd_ref, post_ref):
    u = _rms(x, pre_ref[...]).astype(BF16)
    acc = jnp.zeros(x.shape, F32)
    for sl in (slice(0, FFN_SPLIT), slice(FFN_SPLIT, D_FF)):
        g = _dot(u, wg_ref[:, sl])
        h = (g * jax.nn.sigmoid(g)) * _dot(u, wu_ref[:, sl])
        acc = acc + _dot(h.astype(BF16), wd_ref[sl, :])
    return x + 0.5 * _rms(acc, post_ref[...])


def _ffn_kernel(x_ref, pre_ref, wg_ref, wu_ref, wd_ref, post_ref, o_ref):
    o_ref[...] = _ffn_block(x_ref[...], pre_ref, wg_ref, wu_ref, wd_ref, post_ref)


def _ffn(x, pre, wg, wu, wd, post, *, tm):
    n = x.shape[0]
    return pl.pallas_call(
        _ffn_kernel,
        grid=(n // tm,),
        in_specs=[pl.BlockSpec((tm, D_MODEL), lambda i: (i, 0)),
                  _resident((1, D_MODEL)), _resident((D_MODEL, D_FF)), _resident((D_MODEL, D_FF)),
                  _resident((D_FF, D_MODEL)), _resident((1, D_MODEL))],
        out_specs=pl.BlockSpec((tm, D_MODEL), lambda i: (i, 0)),
        out_shape=jax.ShapeDtypeStruct((n, D_MODEL), F32),
        compiler_params=_params("parallel"),
        name="ffn",
    )(x, pre, wg, wu, wd, post)


N_MAIN = 2 * D_RNN + Q_LORA + KV_LORA + 2 * D_MODEL
M_YR = D_RNN
M_CQ = 2 * D_RNN
M_KV = M_CQ + Q_LORA
M_GA = M_KV + KV_LORA
M_GB = M_GA + D_MODEL


def _proj_kernel(*refs, prompt):
    (x_ref, pre_ref, wmain_ref, qn_ref, kvn_ref, wkr_ref, cosk_ref, sink_ref), refs = refs[:8], refs[8:]
    if prompt:
        (wqrt_ref, wqrtrot_ref, wuqt_ref, cosqt_ref, sinqt_ref, wuk_ref, wuvt_ref), refs = refs[:7], refs[7:]
    else:
        (wuq_ref, wqr_ref, wqrrot_ref, cos512_ref, sin512_ref), refs = refs[:5], refs[5:]
    (xr_ref, gy_ref, ckv_ref, kr_ref, ga_ref, gb_ref), refs = refs[:6], refs[6:]

    u = _rms(x_ref[0], pre_ref[...]).astype(BF16)
    xr_ref[0] = _dot(u, wmain_ref[:, 0:M_YR])
    gy_ref[0] = jax.nn.gelu(_dot(u, wmain_ref[:, M_YR:M_CQ])).astype(BF16)
    ga_ref[0] = jax.nn.sigmoid(_dot(u, wmain_ref[:, M_GA:M_GB])).astype(BF16)
    gb_ref[0] = jax.nn.sigmoid(_dot(u, wmain_ref[:, M_GB:N_MAIN])).astype(BF16)
    cq = _rms(_dot(u, wmain_ref[:, M_CQ:M_KV]), qn_ref[...]).astype(BF16)
    ckv = _rms(_dot(u, wmain_ref[:, M_KV:M_GA]), kvn_ref[...])
    ckv_ref[0] = ckv
    ckv_b = ckv.astype(BF16)
    kr2 = _dot(u, wkr_ref[...])
    kr = kr2[:, :HEAD_PAD] * cosk_ref[...] + kr2[:, HEAD_PAD:] * sink_ref[...]
    kr_ref[0] = kr[:, :QK_ROPE]

    if prompt:
        qt_ref, k_ref, vt_ref = refs
        tm = cq.shape[0]
        k_nope = _dot(ckv_b, wuk_ref[...])
        qr_t = _dot_nt(wqrt_ref[...], cq)
        qrot_t = _dot_nt(wqrtrot_ref[...], cq)
        qn_t = _dot_nt(wuqt_ref[...], cq) * Q_SCALE
        v_t = _dot_nt(wuvt_ref[...], ckv_b)
        cos_t = cosqt_ref[...]
        sin_t = sinqt_ref[...]
        q_pad = jnp.zeros((HEAD_PAD - QK_ROPE - QK_NOPE, tm), BF16)
        ones_row = (lax.broadcasted_iota(jnp.int32, (HEAD_PAD - V_HEAD, tm), 0) == 0).astype(BF16)
        for h in range(N_HEADS):
            r0 = h * HEAD_PAD
            k_ref[0, h] = (k_nope[:, r0:r0 + HEAD_PAD] + kr).astype(BF16)
            rope = slice(h * QK_ROPE, (h + 1) * QK_ROPE)
            qt_ref[0, r0:r0 + QK_ROPE, :] = (qr_t[rope] * cos_t + qrot_t[rope] * sin_t).astype(BF16)
            qt_ref[0, r0 + QK_ROPE:r0 + QK_ROPE + QK_NOPE, :] = qn_t[h * QK_NOPE:(h + 1) * QK_NOPE].astype(BF16)
            qt_ref[0, r0 + QK_ROPE + QK_NOPE:r0 + HEAD_PAD, :] = q_pad
            vt_ref[0, r0:r0 + V_HEAD, :] = v_t[h * V_HEAD:(h + 1) * V_HEAD].astype(BF16)
            vt_ref[0, r0 + V_HEAD:r0 + HEAD_PAD, :] = ones_row
    else:
        qn_out_ref, qr_out_ref = refs
        qn_out_ref[0] = (_dot(cq, wuq_ref[...]) * Q_SCALE).astype(BF16)
        qr_out_ref[0] = (_dot(cq, wqr_ref[...]) * cos512_ref[...]
                         + _dot(cq, wqrrot_ref[...]) * sin512_ref[...]).astype(BF16)


def _proj(x, w, tabs, *, prompt, tm):
    b, s, _ = x.shape
    row = lambda d: pl.BlockSpec((1, tm, d), lambda bi, i: (bi, i, 0))
    tab = lambda d: pl.BlockSpec((tm, d), lambda bi, i: (i, 0))
    tab_t = pl.BlockSpec((QK_ROPE, tm), lambda bi, i: (0, i))
    in_specs = [row(D_MODEL), _resident((1, D_MODEL)), _resident((D_MODEL, N_MAIN)), _resident((1, Q_LORA)),
                _resident((1, KV_LORA)), _resident((D_MODEL, 2 * HEAD_PAD)), tab(HEAD_PAD), tab(HEAD_PAD)]
    args = [x, w["mix_pre"], w["w_main"], w["q_norm"], w["kv_norm"], w["w_kr2"], tabs["cos_k"], tabs["sin_k"]]
    out_specs = [row(D_RNN), row(D_RNN), row(KV_LORA), row(QK_ROPE), row(D_MODEL), row(D_MODEL)]
    out_shape = [jax.ShapeDtypeStruct((b, s, D_RNN), F32), jax.ShapeDtypeStruct((b, s, D_RNN), BF16),
                 jax.ShapeDtypeStruct((b, s, KV_LORA), F32), jax.ShapeDtypeStruct((b, s, QK_ROPE), F32),
                 jax.ShapeDtypeStruct((b, s, D_MODEL), BF16), jax.ShapeDtypeStruct((b, s, D_MODEL), BF16)]
    hp = N_HEADS * HEAD_PAD
    if prompt:
        in_specs += [_resident((N_HEADS * QK_ROPE, Q_LORA)), _resident((N_HEADS * QK_ROPE, Q_LORA)),
                     _resident((N_HEADS * QK_NOPE, Q_LORA)), tab_t, tab_t,
                     _resident((KV_LORA, hp)), _resident((N_HEADS * V_HEAD, KV_LORA))]
        args += [w["w_qrt"], w["w_qrt_rot"], w["w_uqt"], tabs["cos_qt"], tabs["sin_qt"], w["w_uk_pad"], w["w_uvt"]]
        lanes_major = pl.BlockSpec((1, hp, tm), lambda bi, i: (bi, 0, i))
        out_specs += [lanes_major, pl.BlockSpec((1, N_HEADS, tm, HEAD_PAD), lambda bi, i: (bi, 0, i, 0)), lanes_major]
        out_shape += [jax.ShapeDtypeStruct((b, hp, s), BF16),
                      jax.ShapeDtypeStruct((b, N_HEADS, s, HEAD_PAD), BF16),
                      jax.ShapeDtypeStruct((b, hp, s), BF16)]
    else:
        nn, nr = N_HEADS * QK_NOPE, N_HEADS * QK_ROPE
        in_specs += [_resident((Q_LORA, nn)), _resident((Q_LORA, nr)), _resident((Q_LORA, nr)), tab(nr), tab(nr)]
        args += [w["w_uq"], w["w_qr"], w["w_qr_rot"], tabs["cos512"], tabs["sin512"]]
        out_specs += [row(nn), row(nr)]
        out_shape += [jax.ShapeDtypeStruct((b, s, nn), BF16), jax.ShapeDtypeStruct((b, s, nr), BF16)]
    return pl.pallas_call(
        functools.partial(_proj_kernel, prompt=prompt),
        grid=(b, s // tm),
        in_specs=in_specs, out_specs=out_specs, out_shape=out_shape,
        compiler_params=_params("parallel", "parallel"),
        name="proj_prompt" if prompt else "proj_sample",
    )(*args)


def _softplus(x):
    return jnp.maximum(x, 0.0) + jnp.log1p(jnp.exp(-jnp.abs(x)))


def _sigmoid(z):
    return 0.5 * jnp.tanh(0.5 * z) + 0.5


def _lru_coeffs(xc, wbd_ref, bab_ref, lam_ref):
    z = _dot(xc.astype(BF16), wbd_ref[...]) + bab_ref[...]
    r = _sigmoid(z[:, :D_RNN])
    gi = _sigmoid(z[:, D_RNN:])
    log_a = -LRU_C * r * _softplus(-lam_ref[...])
    a = jnp.exp(log_a)
    th = jnp.tanh(log_a)
    mult = jnp.sqrt(-2.0 * th / (1.0 - th))
    return a, mult * (gi * xc)


SUBLANES = 8


def _shift_rows(x, tail, k):
    rolled = pltpu.roll(x, k, axis=0)
    head_rows = lax.broadcasted_iota(jnp.int32, tail.shape, 0)
    first = jnp.where(head_rows < k, pltpu.roll(tail, k, axis=0), rolled[:SUBLANES])
    return jnp.concatenate([first, rolled[SUBLANES:]], axis=0)


def _rglru_prompt_kernel(xr_ref, gy_ref, cw_ref, cb_ref, wbd_ref, bab_ref, lam_ref,
                         hg_ref, hlast_ref, convnew_ref, tail_s, a_s, b_s, h_s, hcar, *, ts):
    t = pl.program_id(0)
    nb = xr_ref.shape[0]

    @pl.when(t == 0)
    def _():
        tail_s[...] = jnp.zeros(tail_s.shape, F32)
        hcar[...] = jnp.zeros(hcar.shape, F32)

    for bi in range(nb):
        x = xr_ref[bi]
        tail = tail_s[bi]
        xc = cb_ref[...] + _shift_rows(x, tail, 3) * cw_ref[0:1, :]
        xc = xc + _shift_rows(x, tail, 2) * cw_ref[1:2, :]
        xc = xc + _shift_rows(x, tail, 1) * cw_ref[2:3, :]
        xc = xc + x * cw_ref[3:4, :]
        tail_s[bi] = x[ts - SUBLANES:, :]
        a, b = _lru_coeffs(xc, wbd_ref, bab_ref, lam_ref)
        a_s[bi] = a
        b_s[bi] = b

    def step(i, hs):
        out = []
        for bi in range(nb):
            h = a_s[bi, pl.ds(i, 1), :] * hs[bi] + b_s[bi, pl.ds(i, 1), :]
            h_s[bi, pl.ds(i, 1), :] = h
            out.append(h)
        return tuple(out)

    hs = lax.fori_loop(0, ts, step, tuple(hcar[bi] for bi in range(nb)), unroll=8)
    for bi in range(nb):
        hcar[bi] = hs[bi]
        hg_ref[bi] = (h_s[bi] * gy_ref[bi].astype(F32)).astype(BF16)

    @pl.when(t == pl.num_programs(0) - 1)
    def _():
        for bi in range(nb):
            hlast_ref[bi] = hs[bi]
            convnew_ref[bi] = xr_ref[bi, ts - (CONV_W - 1):, :]


def _rglru_prompt(xr, gy, w, *, ts):
    b, s, _ = xr.shape
    rows = pl.BlockSpec((b, ts, D_RNN), lambda t: (0, t, 0))
    whole = lambda d: pl.BlockSpec((b, d, D_RNN), lambda t: (0, 0, 0))
    return pl.pallas_call(
        functools.partial(_rglru_prompt_kernel, ts=ts),
        grid=(s // ts,),
        in_specs=[rows, rows, _resident((CONV_W, D_RNN)), _resident((1, D_RNN)), _resident((D_RNN, 2 * D_RNN)),
                  _resident((1, 2 * D_RNN)), _resident((1, D_RNN))],
        out_specs=[rows, whole(1), whole(CONV_W - 1)],
        out_shape=[jax.ShapeDtypeStruct((b, s, D_RNN), BF16), jax.ShapeDtypeStruct((b, 1, D_RNN), F32),
                   jax.ShapeDtypeStruct((b, CONV_W - 1, D_RNN), F32)],
        scratch_shapes=[pltpu.VMEM((b, SUBLANES, D_RNN), F32), pltpu.VMEM((b, ts, D_RNN), F32),
                        pltpu.VMEM((b, ts, D_RNN), F32), pltpu.VMEM((b, ts, D_RNN), F32),
                        pltpu.VMEM((b, 1, D_RNN), F32)],
        compiler_params=_params("arbitrary"),
        name="rglru_prompt",
    )(xr, gy, w["conv_w"], w["conv_b"], w["w_lru"], w["b_lru"], w["lam"])


def _rglru_sample_kernel(xr_ref, gy_ref, sc_ref, h0_ref, cw_ref, cb_ref, wbd_ref, bab_ref, lam_ref,
                         hg_ref, hnew_ref, convnew_ref):
    x = xr_ref[...]
    xc = cb_ref[...] + sc_ref[0] * cw_ref[0:1, :]
    xc = xc + sc_ref[1] * cw_ref[1:2, :]
    xc = xc + sc_ref[2] * cw_ref[2:3, :]
    xc = xc + x * cw_ref[3:4, :]
    a, b = _lru_coeffs(xc, wbd_ref, bab_ref, lam_ref)
    h = a * h0_ref[...] + b
    hnew_ref[...] = h
    hg_ref[...] = (h * gy_ref[...].astype(F32)).astype(BF16)
    convnew_ref[0] = sc_ref[1]
    convnew_ref[1] = sc_ref[2]
    convnew_ref[2] = x


def _rglru_sample(xr, gy, sc, h0, w):
    n = xr.shape[0]
    return pl.pallas_call(
        _rglru_sample_kernel,
        out_shape=[jax.ShapeDtypeStruct((n, D_RNN), BF16), jax.ShapeDtypeStruct((n, D_RNN), F32),
                   jax.ShapeDtypeStruct((CONV_W - 1, n, D_RNN), F32)],
        compiler_params=pltpu.CompilerParams(vmem_limit_bytes=VMEM_LIMIT_BYTES),
        name="rglru_sample",
    )(xr, gy, sc, h0, w["conv_w"], w["conv_b"], w["w_lru"], w["b_lru"], w["lam"])


HEADS_PER_STEP = 4


def _attn_prompt_kernel(qt_ref, k_ref, vt_ref, o_ref, m_s, acc_s, s_a, s_b, *, t):
    qq = pl.program_id(2)
    heads = range(HEADS_PER_STEP)
    m_s[...] = jnp.full(m_s.shape, -jnp.inf, F32)
    acc_s[...] = jnp.zeros(acc_s.shape, F32)

    def rows(hh):
        return slice(hh * HEAD_PAD, (hh + 1) * HEAD_PAD)

    def scores(stage, j, hh):
        k0 = pl.multiple_of(j * t, t)
        return _dot(k_ref[0, hh, pl.ds(k0, t), :], qt_ref[0, rows(hh), stage * t:(stage + 1) * t])

    def consume(stage, s, j, hh, masked=False):
        k0 = pl.multiple_of(j * t, t)
        if masked:
            kpos = lax.broadcasted_iota(jnp.int32, (t, t), 0)
            qpos = lax.broadcasted_iota(jnp.int32, (t, t), 1)
            s = jnp.where(kpos <= qpos, s, -jnp.inf)
        m_old = m_s[stage, hh]
        m_new = jnp.maximum(m_old, jnp.max(s, axis=0, keepdims=True))
        p = jnp.exp2(s - m_new).astype(BF16)
        acc_s[stage, hh] = (jnp.exp2(m_old - m_new) * acc_s[stage, hh]
                            + _dot(vt_ref[0, rows(hh), pl.ds(k0, t)], p))
        m_s[stage, hh] = m_new

    def step(stage, j, src, dst, nxt):
        for hh in heads:
            dst[hh] = scores(nxt[0], nxt[1], hh)
            consume(stage, src[hh], j, hh)

    def finish(stage):
        o_t = jnp.concatenate(
            [acc_s[stage, hh][:V_HEAD] * (1.0 / acc_s[stage, hh][V_HEAD:V_HEAD + 1]) for hh in heads], axis=0)
        o_ref[0, stage * t:(stage + 1) * t, :] = o_t.T.astype(BF16)

    for hh in heads:
        s_a[hh] = scores(0, 0, hh)

    def pair0(i, c):
        step(0, 2 * i, s_a, s_b, (0, 2 * i + 1))
        step(0, 2 * i + 1, s_b, s_a, (0, 2 * i + 2))
        return c

    def run_pairs(pair):
        lax.fori_loop(0, qq // 2, lambda i, c: pair(2 * i + 1, pair(2 * i, c)), 0)

        @pl.when(qq % 2 == 1)
        def _():
            pair(qq - 1, 0)

    run_pairs(pair0)
    for hh in heads:
        s_b[hh] = scores(1, 0, hh)
        consume(0, s_a[hh], 2 * qq, hh, masked=True)
    finish(0)

    def pair1(i, c):
        step(1, 2 * i, s_b, s_a, (1, 2 * i + 1))
        step(1, 2 * i + 1, s_a, s_b, (1, 2 * i + 2))
        return c

    run_pairs(pair1)
    step(1, 2 * qq, s_b, s_a, (1, 2 * qq + 1))
    for hh in heads:
        consume(1, s_a[hh], 2 * qq + 1, hh, masked=True)
    finish(1)


def _attn_prompt(qt, k, vt, *, t):
    b, h, s, _ = k.shape
    hps = HEADS_PER_STEP
    assert (hps * V_HEAD) % LANES == 0 and h % hps == 0 and s % (2 * t) == 0
    return pl.pallas_call(
        functools.partial(_attn_prompt_kernel, t=t),
        grid=(b, h // hps, s // (2 * t)),
        in_specs=[pl.BlockSpec((1, hps * HEAD_PAD, 2 * t), lambda bi, hi, i: (bi, hi, i)),
                  pl.BlockSpec((1, hps, s, HEAD_PAD), lambda bi, hi, i: (bi, hi, 0, 0)),
                  pl.BlockSpec((1, hps * HEAD_PAD, s), lambda bi, hi, i: (bi, hi, 0))],
        out_specs=pl.BlockSpec((1, 2 * t, hps * V_HEAD), lambda bi, hi, i: (bi, i, hi)),
        out_shape=jax.ShapeDtypeStruct((b, s, h * V_HEAD), BF16),
        scratch_shapes=[pltpu.VMEM((2, hps, 1, t), F32), pltpu.VMEM((2, hps, HEAD_PAD, t), F32),
                        pltpu.VMEM((hps, t, t), F32), pltpu.VMEM((hps, t, t), F32)],
        compiler_params=_params("parallel", "parallel", "arbitrary"),
        name="attn_prompt",
    )(qt, k, vt)


def _absorb_kernel(qn_ref, wukt_ref, qa_ref):
    qn = qn_ref[...]
    for h in range(N_HEADS):
        qa_ref[:, h * KV_LORA:(h + 1) * KV_LORA] = _dot(
            qn[:, h * QK_NOPE:(h + 1) * QK_NOPE], wukt_ref[h]).astype(BF16)


def _absorb(qn, wukt3):
    n = qn.shape[0]
    return pl.pallas_call(
        _absorb_kernel,
        out_shape=jax.ShapeDtypeStruct((n, N_HEADS * KV_LORA), BF16),
        compiler_params=pltpu.CompilerParams(vmem_limit_bytes=VMEM_LIMIT_BYTES),
        name="absorb",
    )(qn, wukt3)


def _decode_kernel(pt_ref, qa_ref, qr_ref, cn_ref, kn_ref, pool_ckv, pool_krt, o_ref,
                   ckv_buf, krt_buf, page_a, page_b, p_a, p_b, tail_a, tail_b, l_a, l_b, sems, *, n_seq, n_pages):
    b = pl.program_id(0)
    slot = lax.rem(b, 2)
    set_a = (page_a, p_a, tail_a, l_a)
    set_b = (page_b, p_b, tail_b, l_b)

    def page_copies(page, sl, p):
        return (pltpu.make_async_copy(pool_ckv.at[page], ckv_buf.at[sl, p], sems.at[sl, 0]),
                pltpu.make_async_copy(pool_krt.at[page], krt_buf.at[sl, p], sems.at[sl, 1]))

    def start_row(row, sl):
        def body(p, c):
            for cp in page_copies(pt_ref[row * n_pages + p], sl, p):
                cp.start()
            return c
        lax.fori_loop(0, n_pages, body, 0)

    @pl.when(b == 0)
    def _():
        start_row(0, 0)
        page_b[...] = jnp.zeros(page_b.shape, BF16)
        p_b[...] = jnp.zeros(p_b.shape, BF16)
        tail_b[...] = jnp.zeros(tail_b.shape, F32)
        l_b[...] = jnp.ones(l_b.shape, F32)

    @pl.when(b + 1 < n_seq)
    def _():
        start_row(b + 1, 1 - slot)

    @pl.when(b < n_seq)
    def _():
        for p in range(n_pages):
            for cp in page_copies(0, slot, p):
                cp.wait()

    def score(cur):
        page_c, p_c, tail_c, l_c = cur
        qa = qa_ref[0]
        qr = qr_ref[0]
        s_pairs = []
        for p in range(0, n_pages, 2):
            pair = jnp.concatenate([ckv_buf[slot, p], ckv_buf[slot, p + 1]], axis=0).astype(BF16)
            page_c[p] = pair[:PAGE_SIZE]
            page_c[p + 1] = pair[PAGE_SIZE:]
            kr_pair = jnp.concatenate([krt_buf[slot, p], krt_buf[slot, p + 1]], axis=1).astype(BF16)
            s_pairs.append(_dot_nt(qa, pair) + _dot(qr, kr_pair))
        s = jnp.concatenate(s_pairs, axis=1)
        cn = cn_ref[0].astype(BF16).astype(F32)
        kn = kn_ref[0].astype(BF16).astype(F32)
        s_new = (jnp.sum(qa.astype(F32) * cn, axis=-1, keepdims=True)
                 + jnp.sum(qr.astype(F32) * kn, axis=-1, keepdims=True))
        m = jnp.maximum(jnp.max(s, axis=-1, keepdims=True), s_new)
        p_past = jnp.exp2(s - m)
        p_new = jnp.exp2(s_new - m)
        l_c[...] = jnp.sum(p_past, axis=-1, keepdims=True) + p_new
        tail_c[...] = p_new * cn
        p_c[...] = p_past.astype(BF16)

    def attend(prv):
        page_p, p_p, tail_p, l_p = prv
        acc = tail_p[...]
        for p in range(n_pages):
            acc = acc + _dot(p_p[:, p * PAGE_SIZE:(p + 1) * PAGE_SIZE], page_p[p])
        o_ref[0] = (acc / l_p[...]).astype(BF16)

    @pl.when(jnp.logical_and(b < n_seq, slot == 0))
    def _():
        score(set_a)
        attend(set_b)

    @pl.when(jnp.logical_and(b < n_seq, slot == 1))
    def _():
        score(set_b)
        attend(set_a)

    @pl.when(b == n_seq)
    def _():
        attend(set_a if (n_seq - 1) % 2 == 0 else set_b)


def _decode(page_table, qa, qr, ckv_new, kr_new, pool_ckv, pool_krt):
    n, n_pages = page_table.shape
    per_row = lambda d0, d1: pl.BlockSpec((1, d0, d1), lambda bi, pt: (jnp.minimum(bi, n - 1), 0, 0))
    hbm = pl.BlockSpec(memory_space=pl.ANY)
    page_set = [pltpu.VMEM((n_pages, PAGE_SIZE, KV_LORA), BF16)] * 2
    grid_spec = pltpu.PrefetchScalarGridSpec(
        num_scalar_prefetch=1,
        grid=(n + 1,),
        in_specs=[per_row(N_HEADS, KV_LORA), per_row(N_HEADS, QK_ROPE), per_row(1, KV_LORA), per_row(1, QK_ROPE),
                  hbm, hbm],
        out_specs=pl.BlockSpec((1, N_HEADS, KV_LORA), lambda bi, pt: (jnp.maximum(bi - 1, 0), 0, 0)),
        scratch_shapes=[pltpu.VMEM((2, n_pages, PAGE_SIZE, KV_LORA), F32),
                        pltpu.VMEM((2, n_pages, QK_ROPE, PAGE_SIZE), F32),
                        *page_set,
                        *([pltpu.VMEM((N_HEADS, n_pages * PAGE_SIZE), BF16)] * 2),
                        *([pltpu.VMEM((N_HEADS, KV_LORA), F32)] * 2),
                        *([pltpu.VMEM((N_HEADS, 1), F32)] * 2),
                        pltpu.SemaphoreType.DMA((2, 2))],
    )
    return pl.pallas_call(
        functools.partial(_decode_kernel, n_seq=n, n_pages=n_pages),
        grid_spec=grid_spec,
        out_shape=jax.ShapeDtypeStruct((n, N_HEADS, KV_LORA), BF16),
        compiler_params=_params("arbitrary"),
        name="decode",
    )(page_table.reshape(-1), qa, qr, ckv_new, kr_new, pool_ckv, pool_krt)


def _unabsorb_kernel(ol_ref, wuv_ref, o_ref):
    for h2 in range(N_HEADS // 2):
        pair = jnp.zeros((ol_ref.shape[0], LANES), F32)
        for h in (2 * h2, 2 * h2 + 1):
            pair = pair + _dot(ol_ref[:, h * KV_LORA:(h + 1) * KV_LORA], wuv_ref[h])
        o_ref[:, h2 * LANES:(h2 + 1) * LANES] = pair.astype(BF16)


def _unabsorb(o_lat, w_uv_pair):
    n = o_lat.shape[0]
    return pl.pallas_call(
        _unabsorb_kernel,
        out_shape=jax.ShapeDtypeStruct((n, N_HEADS * V_HEAD), BF16),
        compiler_params=pltpu.CompilerParams(vmem_limit_bytes=VMEM_LIMIT_BYTES),
        name="unabsorb",
    )(o_lat, w_uv_pair)


def _back_kernel(x_ref, hg_ref, o_ref, ga_ref, gb_ref, p_ref, wrnn_ref, wattn_ref, wout_ref, mpost_ref,
                 fpre_ref, wg_ref, wu_ref, wd_ref, fpost_ref, pg_ref, pp_ref, ppost_ref, y_ref):
    y_a = _dot(hg_ref[...], wrnn_ref[...])
    y_b = _dot(o_ref[...], wattn_ref[...])
    m = ga_ref[...].astype(F32) * y_a + gb_ref[...].astype(F32) * y_b
    x = x_ref[...] + _rms(_dot(m.astype(BF16), wout_ref[...]), mpost_ref[...])
    x = _ffn_block(x, fpre_ref, wg_ref, wu_ref, wd_ref, fpost_ref)
    e = jax.nn.sigmoid(_dot(x.astype(BF16), pg_ref[...])) * _dot(p_ref[...].astype(BF16), pp_ref[...])
    y_ref[...] = x + _rms(e, ppost_ref[...])


def _back(x, hg, o, ga, gb, p, w, *, tm):
    n = x.shape[0]
    row = lambda d: pl.BlockSpec((tm, d), lambda i: (i, 0))
    weights = [w["w_branch_rnn"], w["w_branch_attn"], w["w_out"], w["mix_post"], *w["ffn2"],
               w["ple_gate"], w["ple_proj"], w["ple_post"]]
    return pl.pallas_call(
        _back_kernel,
        grid=(n // tm,),
        in_specs=[row(D_MODEL), row(D_RNN), row(D_MODEL), row(D_MODEL), row(D_MODEL), row(D_PLE)]
                 + [_resident(a.shape) for a in weights],
        out_specs=row(D_MODEL),
        out_shape=jax.ShapeDtypeStruct((n, D_MODEL), F32),
        compiler_params=_params("parallel"),
        name="back",
    )(x, hg, o, ga, gb, p, *weights)


def _rot_cols(w):
    half = QK_ROPE // 2
    return jnp.concatenate([-w[..., half:], w[..., :half]], axis=-1)


def _pad_axis(a, axis, before, after):
    pads = [(0, 0)] * a.ndim
    pads[axis] = (before, after)
    return jnp.pad(a, pads)


def _prep_layer(i, ffn1_pre, ffn1_w_gate, ffn1_w_up, ffn1_w_down, ffn1_post, mix_pre, w_in, conv_w, conv_b,
                lru_w_a, lru_b_a, lru_w_i, lru_b_i, lru_lambda, w_branch_rnn, q_norm, w_uq, w_qr, kv_norm, w_uk,
                w_uv, w_branch_attn, w_out, mix_post, ffn2_pre, ffn2_w_gate, ffn2_w_up, ffn2_w_down, ffn2_post,
                ple_gate, ple_proj, ple_post):
    vec = lambda a: a[i].reshape(1, -1)
    w_in_i = w_in[i]
    w_kr = w_in_i[:, O_KV:O_KR]
    pad_r = HEAD_PAD - QK_ROPE - QK_NOPE
    w_qr_i, w_uq_i = w_qr[i], w_uq[i]
    same_block = (jnp.arange(D_RNN)[:, None] // RNN_BLOCK) == (jnp.arange(D_RNN)[None, :] // RNN_BLOCK)
    blockdiag = lambda wb: jnp.where(same_block, jnp.tile(wb.reshape(D_RNN, RNN_BLOCK), (1, RNN_BLOCKS)), 0.0)
    return {
        "ffn1": (vec(ffn1_pre), ffn1_w_gate[i].astype(BF16), ffn1_w_up[i].astype(BF16),
                 ffn1_w_down[i].astype(BF16), vec(ffn1_post)),
        "ffn2": (vec(ffn2_pre), ffn2_w_gate[i].astype(BF16), ffn2_w_up[i].astype(BF16),
                 ffn2_w_down[i].astype(BF16), vec(ffn2_post)),
        "mix_pre": vec(mix_pre),
        "w_main": jnp.concatenate([w_in_i[:, :O_KV], w_in_i[:, O_KR:]], axis=1).astype(BF16),
        "w_kr2": jnp.concatenate([_pad_axis(w_kr, 1, 0, HEAD_PAD - QK_ROPE),
                                  _pad_axis(_rot_cols(w_kr), 1, 0, HEAD_PAD - QK_ROPE)], axis=1).astype(BF16),
        "q_norm": vec(q_norm), "kv_norm": vec(kv_norm),
        "w_qrt": jnp.transpose(w_qr_i, (1, 2, 0)).reshape(N_HEADS * QK_ROPE, Q_LORA).astype(BF16),
        "w_qrt_rot": jnp.transpose(_rot_cols(w_qr_i), (1, 2, 0)).reshape(N_HEADS * QK_ROPE, Q_LORA).astype(BF16),
        "w_uqt": jnp.transpose(w_uq_i, (1, 2, 0)).reshape(N_HEADS * QK_NOPE, Q_LORA).astype(BF16),
        "w_uk_pad": _pad_axis(w_uk[i], 2, QK_ROPE, pad_r).reshape(KV_LORA, N_HEADS * HEAD_PAD).astype(BF16),
        "w_uvt": jnp.transpose(w_uv[i], (1, 2, 0)).reshape(N_HEADS * V_HEAD, KV_LORA).astype(BF16),
        "w_uv_pair": jnp.transpose(
            jnp.where((jnp.arange(N_HEADS) % 2 == 1)[None, :, None], _pad_axis(w_uv[i], 2, V_HEAD, 0),
                      _pad_axis(w_uv[i], 2, 0, V_HEAD)), (1, 0, 2)).astype(BF16),
        "w_uq": w_uq_i.reshape(Q_LORA, N_HEADS * QK_NOPE).astype(BF16),
        "w_qr": w_qr_i.reshape(Q_LORA, N_HEADS * QK_ROPE).astype(BF16),
        "w_qr_rot": _rot_cols(w_qr_i).reshape(Q_LORA, N_HEADS * QK_ROPE).astype(BF16),
        "w_ukt3": jnp.transpose(w_uk[i], (1, 2, 0)).astype(BF16),
        "conv_w": conv_w[i], "conv_b": vec(conv_b),
        "w_lru": jnp.concatenate([blockdiag(lru_w_a[i]), blockdiag(lru_w_i[i])], axis=1).astype(BF16),
        "b_lru": jnp.concatenate([lru_b_a[i].reshape(1, -1), lru_b_i[i].reshape(1, -1)], axis=1),
        "lam": vec(lru_lambda),
        "w_branch_rnn": w_branch_rnn[i].astype(BF16),
        "w_branch_attn": w_branch_attn[i].astype(BF16),
        "w_out": w_out[i].astype(BF16), "mix_post": vec(mix_post),
        "ple_gate": ple_gate[i].astype(BF16), "ple_proj": ple_proj[i].astype(BF16), "ple_post": vec(ple_post),
    }


def _rope_tables(pos, *, prompt):
    half = QK_ROPE // 2
    freqs = ROPE_THETA ** (-jnp.arange(half, dtype=F32) / half)
    ang = pos.astype(F32)[:, None] * freqs[None, :]
    cos32 = jnp.tile(jnp.cos(ang), (1, 2))
    sin32 = jnp.tile(jnp.sin(ang), (1, 2))
    n = pos.shape[0]
    tabs = {"cos_k": _pad_axis(cos32, 1, 0, HEAD_PAD - QK_ROPE), "sin_k": _pad_axis(sin32, 1, 0, HEAD_PAD - QK_ROPE)}
    if prompt:
        tabs["cos_qt"] = Q_SCALE * cos32.T
        tabs["sin_qt"] = Q_SCALE * sin32.T
    else:
        tabs["cos512"] = Q_SCALE * jnp.tile(cos32, (1, N_HEADS))
        tabs["sin512"] = Q_SCALE * jnp.tile(sin32, (1, N_HEADS))
    return tabs


def _tile(n, pref):
    return pref if n % pref == 0 else n


def _layer_prompt(x, p, w):
    b, s, _ = x.shape
    n = b * s
    tabs = _rope_tables(jnp.arange(s, dtype=jnp.int32), prompt=True)
    x1 = _ffn(x.reshape(n, D_MODEL), *w["ffn1"], tm=_tile(n, 1024))
    xr, gy, ckv, krope, ga, gb, qt, k, vt = _proj(x1.reshape(b, s, D_MODEL), w, tabs, prompt=True, tm=_tile(s, 256))
    hg, h_last, conv_new = _rglru_prompt(xr, gy, w, ts=_tile(s, 256))
    o = _attn_prompt(qt, k, vt, t=_tile(s, 512))
    y = _back(x1, hg.reshape(n, D_RNN), o.reshape(n, D_MODEL), ga.reshape(n, D_MODEL), gb.reshape(n, D_MODEL),
              p.reshape(n, D_PLE), w, tm=_tile(n, 512))
    return y.reshape(b, s, D_MODEL), (ckv, krope, h_last.reshape(b, D_RNN), conv_new)


def _layer_sample(x, p, h0, conv_buf, pool_ckv, pool_kr, page_table, w):
    n, s, _ = x.shape
    past_len = page_table.shape[1] * PAGE_SIZE
    tabs = _rope_tables(jnp.full((n,), past_len, jnp.int32), prompt=False)
    x1 = _ffn(x.reshape(n, D_MODEL), *w["ffn1"], tm=n)
    xr, gy, ckv, krope, ga, gb, qn, qr = _proj(x1.reshape(1, n, D_MODEL), w, tabs, prompt=False, tm=n)
    hg, h_new, conv_new = _rglru_sample(xr[0], gy[0], jnp.transpose(conv_buf, (1, 0, 2)), h0, w)
    qa = _absorb(qn[0], w["w_ukt3"]).reshape(n, N_HEADS, KV_LORA)
    o_lat = _decode(page_table, qa, qr.reshape(n, N_HEADS, QK_ROPE), ckv.reshape(n, 1, KV_LORA),
                    krope.reshape(n, 1, QK_ROPE), pool_ckv, jnp.transpose(pool_kr, (0, 2, 1)))
    o = _unabsorb(o_lat.reshape(n, N_HEADS * KV_LORA), w["w_uv_pair"])
    y = _back(x1, hg, o, ga[0], gb[0], p.reshape(n, D_PLE), w, tm=n)
    return (y.reshape(n, s, D_MODEL),
            (ckv.reshape(n, s, KV_LORA), krope.reshape(n, s, QK_ROPE), h_new, jnp.transpose(conv_new, (1, 0, 2))))


def kernel(x_prompt, x_sample, p_prompt, p_sample, cache_ckv, cache_krope, state_h, state_conv, page_table,
           ffn1_pre, ffn1_w_gate, ffn1_w_up, ffn1_w_down, ffn1_post, mix_pre, w_in, conv_w, conv_b, lru_w_a,
           lru_b_a, lru_w_i, lru_b_i, lru_lambda, w_branch_rnn, q_norm, w_uq, w_qr, kv_norm, w_uk, w_uv,
           w_branch_attn, w_out, mix_post, ffn2_pre, ffn2_w_gate, ffn2_w_up, ffn2_w_down, ffn2_post, ple_gate,
           ple_proj, ple_post):
    assert x_sample.shape[1] == 1, "the sample group carries one new token per sequence"
    depth = ffn1_pre.shape[0]
    hp, hs = x_prompt, x_sample
    st_p, st_s = [], []
    for i in range(depth):
        w = _prep_layer(i, ffn1_pre, ffn1_w_gate, ffn1_w_up, ffn1_w_down, ffn1_post, mix_pre, w_in, conv_w, conv_b,
                        lru_w_a, lru_b_a, lru_w_i, lru_b_i, lru_lambda, w_branch_rnn, q_norm, w_uq, w_qr, kv_norm,
                        w_uk, w_uv, w_branch_attn, w_out, mix_post, ffn2_pre, ffn2_w_gate, ffn2_w_up, ffn2_w_down,
                        ffn2_post, ple_gate, ple_proj, ple_post)
        hp, sp = _layer_prompt(hp, p_prompt[i], w)
        hs, ss = _layer_sample(hs, p_sample[i], state_h[i], state_conv[i], cache_ckv[i], cache_krope[i],
                               page_table, w)
        st_p.append(sp)
        st_s.append(ss)
    stack = lambda sts, k: jnp.stack([s[k] for s in sts])
    return (hp, hs, stack(st_p, 0), stack(st_p, 1), stack(st_p, 2), stack(st_p, 3),
            stack(st_s, 0), stack(st_s, 1), stack(st_s, 2), stack(st_s, 3))
```
